```python
import math
import jax
import jax.numpy as jnp
from jax import lax
import numpy as np

D_MODEL = 4096
BATCH = 2
SEQ = 4096
DEPTH = 4

N_MIXERS = 2
N_S5_LAYERS = (DEPTH + N_MIXERS - 1) // N_MIXERS
N_HGRN_LAYERS = DEPTH // N_MIXERS
S5_GROUP = 16
S5_GROUPS = D_MODEL // S5_GROUP
S5_STATE = 64
S5_DT_MIN = 1e-3
S5_DT_MAX = 1e-1
S5_EIG_CLIP = 1e-4
HGRN_HEAD_DIM = 128
HGRN_HEADS = D_MODEL // HGRN_HEAD_DIM
HGRN_CHUNK = 64
D_FF = -(-8 * D_MODEL // (3 * 256)) * 256
RMS_EPS = 1e-6

kernel_name = "hybrid_s5_hgrn2_sandwich_trunk"


def rms_norm(x, gain):
    xf = x.astype(jnp.float32)
    y = xf * lax.rsqrt(jnp.mean(xf * xf, axis=-1, keepdims=True) + RMS_EPS)
    return (y * gain.astype(jnp.float32)).astype(x.dtype)


def _complex_affine_combine(left, right):
    ar1, ai1, br1, bi1 = left
    ar2, ai2, br2, bi2 = right
    return (ar2 * ar1 - ai2 * ai1,
            ar2 * ai1 + ai2 * ar1,
            ar2 * br1 - ai2 * bi1 + br2,
            ar2 * bi1 + ai2 * br1 + bi2)


def s5_mixer(u, a_re, a_im, log_dt, b_re, b_im, c_re, c_im, d_skip, w_glu):
    f32 = jnp.float32
    bsz, t, d = u.shape
    lam_re = jnp.minimum(a_re.astype(f32), -S5_EIG_CLIP)
    lam_im = a_im.astype(f32)
    dt = jnp.exp(log_dt.astype(f32))[:, None]
    mag = jnp.exp(lam_re * dt)
    abar_re = mag * jnp.cos(lam_im * dt)
    abar_im = mag * jnp.sin(lam_im * dt)
    denom = lam_re * lam_re + lam_im * lam_im
    z_re = ((abar_re - 1.0) * lam_re + abar_im * lam_im) / denom
    z_im = (abar_im * lam_re - (abar_re - 1.0) * lam_im) / denom
    br = b_re.astype(f32)
    bi = b_im.astype(f32)
    bbar_re = z_re[..., None] * br - z_im[..., None] * bi
    bbar_im = z_re[..., None] * bi + z_im[..., None] * br
    ug = u.astype(f32).reshape(bsz, t, S5_GROUPS, S5_GROUP)
    bu_re = jnp.einsum('btgh,gph->btgp', ug, bbar_re)
    bu_im = jnp.einsum('btgh,gph->btgp', ug, bbar_im)
    a_seq_re = jnp.broadcast_to(abar_re, (1, t, S5_GROUPS, S5_STATE))
    a_seq_im = jnp.broadcast_to(abar_im, (1, t, S5_GROUPS, S5_STATE))
    _, _, x_re, x_im = lax.associative_scan(
        _complex_affine_combine, (a_seq_re, a_seq_im, bu_re, bu_im), axis=1)
    y = (jnp.einsum('btgp,ghp->btgh', x_re, c_re.astype(f32))
         - jnp.einsum('btgp,ghp->btgh', x_im, c_im.astype(f32))).reshape(bsz, t, d)
    y = y + d_skip.astype(f32) * u.astype(f32)
    y = jax.nn.gelu(y).astype(u.dtype)
    val, gate = jnp.split(y @ w_glu, 2, axis=-1)
    return (val * jax.nn.sigmoid(gate)).astype(u.dtype)


def hgrn2_mixer(h, w_in, lower_bound, g_norm, w_out):
    f32 = jnp.float32
    bsz, t, d = h.shape
    n_chunks = t // HGRN_CHUNK
    q, f, v, g = jnp.split(h @ w_in, 4, axis=-1)
    q = jax.nn.silu(q.astype(f32))
    f = f.astype(f32)
    lb = lower_bound.astype(f32)
    log_forget = jnp.logaddexp(jnp.log(lb), jnp.log1p(-lb) + jax.nn.log_sigmoid(f))
    k = (1.0 - lb) * jax.nn.sigmoid(-f)
    v = v.astype(f32)

    def to_chunks(z):
        return z.reshape(bsz, n_chunks, HGRN_CHUNK, HGRN_HEADS, HGRN_HEAD_DIM).transpose(1, 0, 3, 2, 4)

    causal = jnp.tril(jnp.ones((HGRN_CHUNK, HGRN_CHUNK), dtype=bool))

    def chunk_step(state, inp):
        qc, kc, vc, lc = inp
        cum = jnp.cumsum(lc, axis=2)
        diff = cum[:, :, :, None, :] - cum[:, :, None, :, :]
        decay = jnp.exp(jnp.where(causal[:, :, None], diff, -jnp.inf))
        scores = jnp.einsum('bhtd,bhsd,bhtsd->bhts', qc, kc, decay)
        out = (jnp.einsum('bhts,bhse->bhte', scores, vc)
               + jnp.einsum('bhtd,bhde->bhte', qc * jnp.exp(cum), state))
        cum_last = cum[:, :, -1:, :]
        state = (jnp.exp(cum_last[:, :, 0, :])[..., None] * state
                 + jnp.einsum('bhsd,bhse->bhde', kc * jnp.exp(cum_last - cum), vc))
        return state, out

    state0 = jnp.zeros((bsz, HGRN_HEADS, HGRN_HEAD_DIM, HGRN_HEAD_DIM), f32)
    _, o = lax.scan(chunk_step, state0,
                    (to_chunks(q), to_chunks(k), to_chunks(v), to_chunks(log_forget)))
    o = o.transpose(1, 0, 3, 2, 4).reshape(bsz, t, HGRN_HEADS, HGRN_HEAD_DIM)
    o = o * lax.rsqrt(jnp.mean(o * o, axis=-1, keepdims=True) + RMS_EPS)
    o = o.reshape(bsz, t, d) * g_norm.astype(f32) * jax.nn.silu(g.astype(f32))
    return o.astype(h.dtype) @ w_out


def swiglu_ffn(h, w_gate_up, w_down):
    gate, up = jnp.split(h @ w_gate_up, 2, axis=-1)
    return (jax.nn.silu(gate) * up) @ w_down


def setup_inputs(seed: int = 0) -> dict:
    key = jax.random.key(seed)
    ks = jax.random.split(key, 18)
    f32 = jnp.float32
    nrm = lambda k, shape, scale: scale * jax.random.normal(k, shape, f32)
    x = nrm(ks[0], (BATCH, SEQ, D_MODEL), 1.0)
    norm_gains = 1.0 + nrm(ks[1], (DEPTH, 4, D_MODEL), 0.05)
    s5_a_re = -0.5 + nrm(ks[2], (N_S5_LAYERS, S5_GROUPS, S5_STATE), 0.01)
    s5_a_im = (jnp.pi * jnp.arange(S5_STATE, dtype=f32)
               + nrm(ks[3], (N_S5_LAYERS, S5_GROUPS, S5_STATE), 0.01))
    s5_log_dt = jax.random.uniform(ks[4], (N_S5_LAYERS, S5_GROUPS), f32,
                                   minval=math.log(S5_DT_MIN), maxval=math.log(S5_DT_MAX))
    b_scale = (2.0 * S5_GROUP) ** -0.5
    s5_b_re = nrm(ks[5], (N_S5_LAYERS, S5_GROUPS, S5_STATE, S5_GROUP), b_scale)
    s5_b_im = nrm(ks[6], (N_S5_LAYERS, S5_GROUPS, S5_STATE, S5_GROUP), b_scale)
    c_scale = (2.0 * S5_STATE) ** -0.5
    s5_c_re = nrm(ks[7], (N_S5_LAYERS, S5_GROUPS, S5_GROUP, S5_STATE), c_scale)
    s5_c_im = nrm(ks[8], (N_S5_LAYERS, S5_GROUPS, S5_GROUP, S5_STATE), c_scale)
    s5_d = nrm(ks[9], (N_S5_LAYERS, D_MODEL), 1.0)
    s5_w_glu = nrm(ks[10], (N_S5_LAYERS, D_MODEL, 2 * D_MODEL), D_MODEL ** -0.5)
    hgrn_w_in = nrm(ks[11], (N_HGRN_LAYERS, D_MODEL, 4 * D_MODEL), D_MODEL ** -0.5)
    hgrn_lb_logits = nrm(ks[12], (DEPTH, D_MODEL), 0.1)
    hgrn_g_norm = 1.0 + nrm(ks[13], (N_HGRN_LAYERS, D_MODEL), 0.05)
    hgrn_w_out = nrm(ks[14], (N_HGRN_LAYERS, D_MODEL, D_MODEL), D_MODEL ** -0.5)
    ffn_w_gate_up = nrm(ks[15], (DEPTH, D_MODEL, 2 * D_FF), D_MODEL ** -0.5)
    ffn_w_down = nrm(ks[16], (DEPTH, D_FF, D_MODEL), D_FF ** -0.5)
    return {"x": x, "norm_gains": norm_gains,
            "s5_a_re": s5_a_re, "s5_a_im": s5_a_im, "s5_log_dt": s5_log_dt,
            "s5_b_re": s5_b_re, "s5_b_im": s5_b_im, "s5_c_re": s5_c_re, "s5_c_im": s5_c_im,
            "s5_d": s5_d, "s5_w_glu": s5_w_glu,
            "hgrn_w_in": hgrn_w_in, "hgrn_lb_logits": hgrn_lb_logits,
            "hgrn_g_norm": hgrn_g_norm, "hgrn_w_out": hgrn_w_out,
            "ffn_w_gate_up": ffn_w_gate_up, "ffn_w_down": ffn_w_down}


def reference(x, norm_gains, s5_a_re, s5_a_im, s5_log_dt, s5_b_re, s5_b_im, s5_c_re, s5_c_im,
              s5_d, s5_w_glu, hgrn_w_in, hgrn_lb_logits, hgrn_g_norm, hgrn_w_out,
              ffn_w_gate_up, ffn_w_down):
    lb_probs = jax.nn.softmax(hgrn_lb_logits.astype(jnp.float32), axis=0)
    lb_cum = jnp.cumsum(lb_probs, axis=0)
    lower_bounds = lb_cum - lb_cum[0]
    h = x
    for layer in range(DEPTH):
        gains = norm_gains[layer]
        a = rms_norm(h, gains[0])
        j = layer // N_MIXERS
        if layer % N_MIXERS == 0:
            m = s5_mixer(a, s5_a_re[j], s5_a_im[j], s5_log_dt[j], s5_b_re[j], s5_b_im[j],
                         s5_c_re[j], s5_c_im[j], s5_d[j], s5_w_glu[j])
        else:
            m = hgrn2_mixer(a, hgrn_w_in[j], lower_bounds[layer], hgrn_g_norm[j], hgrn_w_out[j])
        h = h + rms_norm(m, gains[1])
        f = swiglu_ffn(rms_norm(h, gains[2]), ffn_w_gate_up[layer], ffn_w_down[layer])
        h = h + rms_norm(f, gains[3])
    return h
```

```python
import functools

import jax
import jax.numpy as jnp
from jax import lax
from jax.experimental import pallas as pl
from jax.experimental.pallas import tpu as pltpu

F32 = jnp.float32
BF16 = jnp.bfloat16
HIGHEST = lax.Precision.HIGHEST

RMS_EPS = 1e-6
S5_GROUP = 16
S5_STATE = 64
S5_EIG_CLIP = 1e-4
S5_CHUNK = 16
S5_GROUP_BLOCK = 8
HGRN_HEAD = 128
HGRN_CHUNK = 64
HGRN_SUB = 16
LANES = 128
SUBLANES = 8
VMEM_LIMIT = 56 * 1024 * 1024


def _cparams(*sem):
    return pltpu.CompilerParams(dimension_semantics=sem, vmem_limit_bytes=VMEM_LIMIT)


def _tile(n, want):
    t = min(n, want)
    assert n % t == 0, (n, want)
    return t


def _lower_bounds_kernel(logit_ref, out_ref):
    x = logit_ref[...]
    e = jnp.exp(x - jnp.max(x, axis=0, keepdims=True))
    p = e / jnp.sum(e, axis=0, keepdims=True)
    acc = jnp.zeros_like(p[0:1])
    rows = [acc]
    for layer in range(1, x.shape[0]):
        acc = acc + p[layer:layer + 1]
        rows.append(acc)
    out_ref[...] = jnp.concatenate(rows, axis=0)


def _lower_bounds(logits):
    return pl.pallas_call(
        _lower_bounds_kernel,
        out_shape=jax.ShapeDtypeStruct(logits.shape, F32),
        name="lower_bounds",
    )(logits.astype(F32))


def _rms(x, gain):
    return x * lax.rsqrt(jnp.mean(x * x, axis=-1, keepdims=True) + RMS_EPS) * gain


def _norm_kernel(x_ref, g_ref, o_ref):
    o_ref[...] = _rms(x_ref[...], g_ref[...]).astype(o_ref.dtype)


def _norm(x, gain, out_dtype, bm=256):
    m, d = x.shape
    bm = _tile(m, bm)
    row = pl.BlockSpec((bm, d), lambda i: (i, 0))
    vec = pl.BlockSpec((1, d), lambda i: (0, 0))
    return pl.pallas_call(
        _norm_kernel, grid=(m // bm,), in_specs=[row, vec], out_specs=row,
        out_shape=jax.ShapeDtypeStruct((m, d), out_dtype),
        compiler_params=_cparams("parallel"), name="rms_norm",
    )(x, gain.reshape(1, d))


def _resid_norm_kernel(h_ref, m_ref, gpost_ref, gpre_ref, h_out_ref, a_out_ref):
    h = h_ref[...] + _rms(m_ref[...], gpost_ref[...])
    h_out_ref[...] = h
    a_out_ref[...] = _rms(h, gpre_ref[...]).astype(a_out_ref.dtype)


def _resid_kernel(h_ref, m_ref, gpost_ref, h_out_ref):
    h_out_ref[...] = h_ref[...] + _rms(m_ref[...], gpost_ref[...])


def _resid_norm(h, m_out, g_post, g_pre, a_dtype, bm=256):
    m, d = h.shape
    bm = _tile(m, bm)
    row = pl.BlockSpec((bm, d), lambda i: (i, 0))
    vec = pl.BlockSpec((1, d), lambda i: (0, 0))
    if g_pre is None:
        return pl.pallas_call(
            _resid_kernel, grid=(m // bm,), in_specs=[row, row, vec], out_specs=row,
            out_shape=jax.ShapeDtypeStruct((m, d), F32),
            compiler_params=_cparams("parallel"), name="resid",
        )(h, m_out, g_post.reshape(1, d)), None
    return pl.pallas_call(
        _resid_norm_kernel, grid=(m // bm,), in_specs=[row, row, vec, vec],
        out_specs=[row, row],
        out_shape=[jax.ShapeDtypeStruct((m, d), F32), jax.ShapeDtypeStruct((m, d), a_dtype)],
        compiler_params=_cparams("parallel"), name="resid_norm",
    )(h, m_out, g_post.reshape(1, d), g_pre.reshape(1, d))


def _mm_kernel(*refs, n_w, n_vec, epilogue):
    x_ref = refs[0]
    w_refs = refs[1:1 + n_w]
    vec_refs = refs[1 + n_w:1 + n_w + n_vec]
    out_refs = refs[1 + n_w + n_vec:]
    x = x_ref[...]
    accs = [jnp.dot(x, w[...], preferred_element_type=F32) for w in w_refs]
    outs = epilogue(*accs, *[v[...] for v in vec_refs])
    for o_ref, val in zip(out_refs, outs):
        o_ref[...] = val.astype(o_ref.dtype)


def _matmul(x, w, col_starts, n_cols, epilogue, out_dtypes, vecs=(), bm=1024, bn=256, name="matmul"):
    m, kdim = x.shape
    bm = _tile(m, bm)
    bn = _tile(n_cols, bn)
    x_spec = pl.BlockSpec((bm, kdim), lambda i, j: (i, 0))
    w_specs = []
    for s in col_starts:
        assert s % bn == 0
        w_specs.append(pl.BlockSpec((kdim, bn), functools.partial(lambda i, j, off: (0, j + off), off=s // bn)))
    vec_spec = pl.BlockSpec((1, bn), lambda i, j: (0, j))
    out_spec = pl.BlockSpec((bm, bn), lambda i, j: (i, j))
    outs = pl.pallas_call(
        functools.partial(_mm_kernel, n_w=len(col_starts), n_vec=len(vecs), epilogue=epilogue),
        grid=(m // bm, n_cols // bn),
        in_specs=[x_spec] + w_specs + [vec_spec] * len(vecs),
        out_specs=[out_spec] * len(out_dtypes),
        out_shape=[jax.ShapeDtypeStruct((m, n_cols), dt) for dt in out_dtypes],
        compiler_params=_cparams("parallel", "arbitrary"), name=name,
    )(x, *([w] * len(col_starts)), *[v.reshape(1, n_cols) for v in vecs])
    return outs


def _ep_identity(acc):
    return (acc,)


def _ep_silu(acc):
    return (acc * jax.nn.sigmoid(acc),)


def _ep_glu(val, gate):
    return (val * jax.nn.sigmoid(gate),)


def _ep_swiglu(gate, up):
    return (gate * jax.nn.sigmoid(gate) * up,)


def _ep_forget(f, lb):
    log_sig = jnp.minimum(f, 0.0) - jnp.log1p(jnp.exp(-jnp.abs(f)))
    a = jnp.log(lb)
    b = jnp.log1p(-lb) + log_sig
    log_forget = jnp.maximum(a, b) + jnp.log1p(jnp.exp(-jnp.abs(a - b)))
    key = (1.0 - lb) * jax.nn.sigmoid(-f)
    return log_forget, key


def _s5_prep_kernel(are_ref, aim_ref, ldt_ref, btr_ref, bti_ref, cre_ref, cim_ref,
                    mw_ref, o_ref, al_ref):
    gb = are_ref.shape[0]
    h, p, big_l = S5_GROUP, S5_STATE, S5_CHUNK
    lane = lax.broadcasted_iota(jnp.int32, (h, big_l * h), 1)
    zpad = jnp.zeros((h, LANES - p), F32)
    for g in range(gb):
        lam_re = jnp.minimum(are_ref[g], -S5_EIG_CLIP)
        lam_im = aim_ref[g]
        dt = jnp.exp(ldt_ref[g])
        mag = jnp.exp(lam_re * dt)
        ar = mag * jnp.cos(lam_im * dt)
        ai = mag * jnp.sin(lam_im * dt)
        den = lam_re * lam_re + lam_im * lam_im
        z_re = ((ar - 1.0) * lam_re + ai * lam_im) / den
        z_im = (ai * lam_re - (ar - 1.0) * lam_im) / den
        btr, bti = btr_ref[g], bti_ref[g]
        bbr = z_re * btr - z_im * bti
        bbi = z_re * bti + z_im * btr
        cre, cim = cre_ref[g], cim_ref[g]
        pr = jnp.ones_like(ar)
        pi = jnp.zeros_like(ar)
        cp_rows = []
        for tau in range(big_l + 1):
            car = cre * pr - cim * pi
            cai = cre * pi + cim * pr
            if tau < big_l:
                cp_rows.append(jnp.concatenate([car, -cai], axis=-1))
                bar = bbr * pr - bbi * pi
                bai = bbr * pi + bbi * pr
                s = big_l - 1 - tau
                mw_ref[g, s * h:(s + 1) * h, 256:384] = jnp.concatenate([bar, zpad], -1).astype(mw_ref.dtype)
                mw_ref[g, s * h:(s + 1) * h, 384:512] = jnp.concatenate([bai, zpad], -1).astype(mw_ref.dtype)
            if tau >= 1:
                t = tau - 1
                o_ref[g, 0, t * h:(t + 1) * h, :] = jnp.concatenate([car, zpad], -1)
                o_ref[g, 1, t * h:(t + 1) * h, :] = jnp.concatenate([-cai, zpad], -1)
            if tau == big_l:
                al_ref[g, 0:1, :] = jnp.concatenate([pr, zpad[0:1]], -1)
                al_ref[g, 1:2, :] = jnp.concatenate([pi, zpad[0:1]], -1)
            pr, pi = pr * ar - pi * ai, pr * ai + pi * ar
        cp = jnp.concatenate(cp_rows, axis=0)
        bb = jnp.concatenate([bbr, bbi], axis=-1)
        krow = lax.dot_general(bb, cp, (((1,), (1,)), ((), ())), precision=HIGHEST,
                               preferred_element_type=F32)
        for s in range(big_l):
            shifted = krow if s == 0 else pltpu.roll(krow, s * h, axis=1)
            mw_ref[g, s * h:(s + 1) * h, 0:256] = jnp.where(lane >= s * h, shifted, 0.0).astype(mw_ref.dtype)


def _s5_prep(a_re, a_im, log_dt, b_re, b_im, c_re, c_im):
    g, p = a_re.shape
    h, big_l = S5_GROUP, S5_CHUNK
    gb = _tile(g, S5_GROUP_BLOCK)
    row = lambda z: z.astype(F32).reshape(g, 1, p)
    ldt = jnp.broadcast_to(log_dt.astype(F32)[:, None, None], (g, 1, p))
    bt = lambda z: jnp.swapaxes(z.astype(F32), 1, 2)
    vspec = pl.BlockSpec((gb, 1, p), lambda i: (i, 0, 0))
    mspec = pl.BlockSpec((gb, h, p), lambda i: (i, 0, 0))
    return pl.pallas_call(
        _s5_prep_kernel, grid=(g // gb,),
        in_specs=[vspec, vspec, vspec, mspec, mspec, mspec, mspec],
        out_specs=[pl.BlockSpec((gb, big_l * h, 512), lambda i: (i, 0, 0)),
                   pl.BlockSpec((gb, 2, big_l * h, LANES), lambda i: (i, 0, 0, 0)),
                   pl.BlockSpec((gb, 2, LANES), lambda i: (i, 0, 0))],
        out_shape=[jax.ShapeDtypeStruct((g, big_l * h, 512), BF16),
                   jax.ShapeDtypeStruct((g, 2, big_l * h, LANES), F32),
                   jax.ShapeDtypeStruct((g, 2, LANES), F32)],
        compiler_params=_cparams("parallel"), name="s5_prep",
    )(row(a_re), row(a_im), ldt, bt(b_re), bt(b_im), c_re.astype(F32), c_im.astype(F32))


def _s5_main_kernel(u_ref, mw_ref, o_ref, al_ref, y_ref, wre_ref, wim_ref, sre_ref, sim_ref,
                    *, batch, n_chunks):
    gb = u_ref.shape[0]
    nseq = gb * batch
    for g in range(gb):
        r = jnp.dot(u_ref[g], mw_ref[g], preferred_element_type=F32)
        y_ref[g] = r[:, 0:256]
        for b in range(batch):
            rows = slice(b * n_chunks, (b + 1) * n_chunks)
            wre_ref[pl.ds(g * batch + b, n_chunks, stride=nseq), :] = r[rows, 256:384]
            wim_ref[pl.ds(g * batch + b, n_chunks, stride=nseq), :] = r[rows, 384:512]
    ar = jnp.concatenate([al_ref[g, 0:1, :] for g in range(gb) for _ in range(batch)], axis=0)
    ai = jnp.concatenate([al_ref[g, 1:2, :] for g in range(gb) for _ in range(batch)], axis=0)

    def step(c, carry):
        s_re, s_im = carry
        base = pl.multiple_of(c * nseq, nseq)
        sre_ref[pl.ds(base, nseq), :] = s_re
        sim_ref[pl.ds(base, nseq), :] = s_im
        n_re = ar * s_re - ai * s_im + wre_ref[pl.ds(base, nseq), :]
        n_im = ar * s_im + ai * s_re + wim_ref[pl.ds(base, nseq), :]
        return n_re, n_im

    zero = jnp.zeros((nseq, LANES), F32)
    lax.fori_loop(0, n_chunks, step, (zero, zero))
    nt = (((1,), (1,)), ((), ()))
    for g in range(gb):
        for b in range(batch):
            s_re = sre_ref[pl.ds(g * batch + b, n_chunks, stride=nseq), :]
            s_im = sim_ref[pl.ds(g * batch + b, n_chunks, stride=nseq), :]
            ys = (lax.dot_general(s_re, o_ref[g, 0], nt, precision=HIGHEST, preferred_element_type=F32)
                  + lax.dot_general(s_im, o_ref[g, 1], nt, precision=HIGHEST, preferred_element_type=F32))
            y_ref[g, b * n_chunks:(b + 1) * n_chunks, :] += ys


def _s5_core(a, batch, mw, o_mat, al):
    m, d = a.shape
    t = m // batch
    h, big_l = S5_GROUP, S5_CHUNK
    g = d // h
    n_chunks = t // big_l
    gb = _tile(g, S5_GROUP_BLOCK)
    u = (a.reshape(batch, n_chunks, big_l, g, h).transpose(3, 0, 1, 2, 4)
         .reshape(g, batch * n_chunks, big_l * h).astype(BF16))
    nrow = n_chunks * gb * batch
    y = pl.pallas_call(
        functools.partial(_s5_main_kernel, batch=batch, n_chunks=n_chunks),
        grid=(g // gb,),
        in_specs=[pl.BlockSpec((gb, batch * n_chunks, big_l * h), lambda i: (i, 0, 0)),
                  pl.BlockSpec((gb, big_l * h, 512), lambda i: (i, 0, 0)),
                  pl.BlockSpec((gb, 2, big_l * h, LANES), lambda i: (i, 0, 0, 0)),
                  pl.BlockSpec((gb, 2, LANES), lambda i: (i, 0, 0))],
        out_specs=pl.BlockSpec((gb, batch * n_chunks, big_l * h), lambda i: (i, 0, 0)),
        out_shape=jax.ShapeDtypeStruct((g, batch * n_chunks, big_l * h), F32),
        scratch_shapes=[pltpu.VMEM((nrow, LANES), F32)] * 4,
        compiler_params=_cparams("parallel"), name="s5_main",
    )(u, mw, o_mat, al)
    return (y.reshape(g, batch, n_chunks, big_l, h).transpose(1, 2, 3, 0, 4).reshape(m, d))


def _s5_act_kernel(y_ref, a_ref, d_ref, z_ref):
    z_ref[...] = jax.nn.gelu(y_ref[...] + d_ref[...] * a_ref[...]).astype(z_ref.dtype)


def _s5_act(y, a, d_skip, bm=256):
    m, d = y.shape
    bm = _tile(m, bm)
    row = pl.BlockSpec((bm, d), lambda i: (i, 0))
    vec = pl.BlockSpec((1, d), lambda i: (0, 0))
    return pl.pallas_call(
        _s5_act_kernel, grid=(m // bm,), in_specs=[row, row, vec], out_specs=row,
        out_shape=jax.ShapeDtypeStruct((m, d), BF16),
        compiler_params=_cparams("parallel"), name="s5_act",
    )(y, a, d_skip.astype(F32).reshape(1, d))


def _hgrn_core_kernel(q_ref, k_ref, lf_ref, v_ref, sg_ref, gn_ref, o_ref, st_ref, cum_ref, kc_ref, *, heads):
    c_len, sub, hd_dim = HGRN_CHUNK, HGRN_SUB, HGRN_HEAD
    tb = q_ref.shape[0]

    @pl.when(pl.program_id(2) == 0)
    def _():
        st_ref[...] = jnp.zeros_like(st_ref)

    ri = lax.broadcasted_iota(jnp.int32, (c_len, c_len), 0)
    ci = lax.broadcasted_iota(jnp.int32, (c_len, c_len), 1)
    tri = (ri >= ci).astype(F32)
    row = lax.broadcasted_iota(jnp.int32, (c_len, hd_dim), 0)
    row8 = lax.broadcasted_iota(jnp.int32, (SUBLANES, hd_dim), 0)
    lane8 = lax.broadcasted_iota(jnp.int32, (SUBLANES, hd_dim), 1)
    ones = jnp.ones((hd_dim, hd_dim), BF16)
    nt = (((1,), (1,)), ((), ()))
    tn = (((0,), (0,)), ((), ()))
    n_sub = c_len // sub
    assert n_sub == 4

    def chunk_body(c, carry):
        r0 = pl.multiple_of(c * c_len, c_len)
        rows = pl.ds(r0, c_len)
        cum_ref[...] = jnp.dot(tri, lf_ref[rows, :], precision=HIGHEST, preferred_element_type=F32)
        kc_ref[...] = k_ref[rows, :]
        for hd in range(heads):
            ls = slice(hd * hd_dim, (hd + 1) * hd_dim)
            q = q_ref[rows, ls]
            k = kc_ref[:, ls]
            v = v_ref[rows, ls]
            cum = cum_ref[:, ls]
            vb = v.astype(BF16)
            r15, r31, r47, r63 = (cum_ref[i:i + 1, ls] for i in (15, 31, 47, 63))

            st = st_ref[hd]
            q_in = (q * jnp.exp(cum)).astype(BF16)
            k_out = (k * jnp.exp(r63 - cum)).astype(BF16)
            o = lax.dot_general(q_in, st.astype(BF16), nt, preferred_element_type=F32)
            st_ref[hd] = st * jnp.exp(r63) + lax.dot_general(vb, k_out, tn, preferred_element_type=F32)

            zero = jnp.zeros_like(q)
            q_a = jnp.where(row >= 32, q * jnp.exp(jnp.minimum(cum - r31, 0.0)), zero)
            k_a = jnp.where(row < 32, k * jnp.exp(jnp.minimum(r31 - cum, 0.0)), zero)
            q_b1 = jnp.where((row >= 16) & (row < 32), q * jnp.exp(jnp.minimum(cum - r15, 0.0)), zero)
            k_b1 = jnp.where(row < 16, k * jnp.exp(jnp.minimum(r15 - cum, 0.0)), zero)
            q_b2 = jnp.where(row >= 48, q * jnp.exp(jnp.minimum(cum - r47, 0.0)), zero)
            k_b2 = jnp.where((row >= 32) & (row < 48), k * jnp.exp(jnp.minimum(r47 - cum, 0.0)), zero)
            q_cat = jnp.concatenate([q_a, q_b1, q_b2], axis=-1).astype(BF16)
            k_cat = jnp.concatenate([k_a, k_b1, k_b2], axis=-1).astype(BF16)
            scores = lax.dot_general(q_cat, k_cat, nt, preferred_element_type=F32)

            e_tiles = []
            for blk in range(n_sub):
                for half in range(sub // SUBLANES):
                    t0 = blk * sub + half * SUBLANES
                    qt = q[t0:t0 + SUBLANES]
                    ct = cum[t0:t0 + SUBLANES]
                    for s in range(blk * sub, t0 + SUBLANES):
                        ks = kc_ref[s:s + 1, ls]
                        cs = cum_ref[s:s + 1, ls]
                        e = qt * ks * jnp.exp(ct - cs)
                        if s >= t0:
                            e = jnp.where(row8 >= s - t0, e, 0.0)
                        e_tiles.append(e)
            e_all = jnp.concatenate(e_tiles, axis=0).astype(BF16)
            sums = jnp.dot(e_all, ones, preferred_element_type=F32)
            diag_tiles = []
            idx = 0
            for blk in range(n_sub):
                for half in range(sub // SUBLANES):
                    t0 = blk * sub + half * SUBLANES
                    acc = jnp.zeros((SUBLANES, hd_dim), F32)
                    for s in range(blk * sub, t0 + SUBLANES):
                        acc = jnp.where(lane8 == s, sums[idx * SUBLANES:(idx + 1) * SUBLANES], acc)
                        idx += 1
                    diag_tiles.append(acc)
            scores = scores + jnp.concatenate(diag_tiles, axis=0)[:, 0:c_len]
            o = o + jnp.dot(scores.astype(BF16), vb, preferred_element_type=F32)

            o = o * lax.rsqrt(jnp.mean(o * o, axis=-1, keepdims=True) + RMS_EPS)
            o = o * gn_ref[:, ls] * sg_ref[rows, ls].astype(F32)
            o_ref[rows, ls] = o.astype(o_ref.dtype)
        return carry

    lax.fori_loop(0, tb // c_len, chunk_body, 0)


def _hgrn_core(q, k, log_f, v, sg, g_norm, batch, tb=256, heads=2):
    m, d = q.shape
    t = m // batch
    tb = _tile(t, tb)
    heads = _tile(d // HGRN_HEAD, heads)
    wl = heads * HGRN_HEAD
    nt = t // tb
    blk = pl.BlockSpec((tb, wl), lambda b, j, i: (b * nt + i, j))
    vec = pl.BlockSpec((1, wl), lambda b, j, i: (0, j))
    return pl.pallas_call(
        functools.partial(_hgrn_core_kernel, heads=heads),
        grid=(batch, d // wl, nt),
        in_specs=[blk, blk, blk, blk, blk, vec],
        out_specs=blk,
        out_shape=jax.ShapeDtypeStruct((m, d), BF16),
        scratch_shapes=[pltpu.VMEM((heads, HGRN_HEAD, HGRN_HEAD), F32),
                        pltpu.VMEM((HGRN_CHUNK, wl), F32),
                        pltpu.VMEM((HGRN_CHUNK, wl), F32)],
        compiler_params=_cparams("parallel", "parallel", "arbitrary"), name="hgrn_core",
    )(q, k, log_f, v, sg, g_norm.astype(F32).reshape(1, d))


def kernel(x, norm_gains, s5_a_re, s5_a_im, s5_log_dt, s5_b_re, s5_b_im, s5_c_re, s5_c_im, s5_d,
           s5_w_glu, hgrn_w_in, hgrn_lb_logits, hgrn_g_norm, hgrn_w_out, ffn_w_gate_up, ffn_w_down):
    batch, seq, d = x.shape
    depth = norm_gains.shape[0]
    d_ff = ffn_w_down.shape[1]
    m = batch * seq
    gains = norm_gains.astype(F32)
    lower_bounds = _lower_bounds(hgrn_lb_logits)

    h = x.reshape(m, d).astype(F32)
    a = _norm(h, gains[0, 0], F32)
    for layer in range(depth):
        j = layer // 2
        if layer % 2 == 0:
            mw, o_mat, al = _s5_prep(s5_a_re[j], s5_a_im[j], s5_log_dt[j], s5_b_re[j], s5_b_im[j],
                                     s5_c_re[j], s5_c_im[j])
            y = _s5_core(a, batch, mw, o_mat, al)
            z = _s5_act(y, a, s5_d[j])
            (mix,) = _matmul(z, s5_w_glu[j].astype(BF16), (0, d), d, _ep_glu, (F32,), name="s5_glu")
        else:
            w_in = hgrn_w_in[j].astype(BF16)
            (q,) = _matmul(a, w_in, (0,), d, _ep_silu, (F32,), bn=512, name="hgrn_q")
            log_f, key = _matmul(a, w_in, (d,), d, _ep_forget, (F32, F32), vecs=(lower_bounds[layer],),
                                 bn=512, name="hgrn_f")
            (val,) = _matmul(a, w_in, (2 * d,), d, _ep_identity, (F32,), bn=512, name="hgrn_v")
            (sg,) = _matmul(a, w_in, (3 * d,), d, _ep_silu, (BF16,), bn=512, name="hgrn_g")
            o = _hgrn_core(q, key, log_f, val, sg, hgrn_g_norm[j], batch)
            (mix,) = _matmul(o, hgrn_w_out[j].astype(BF16), (0,), d, _ep_identity, (F32,), bn=512,
                             name="hgrn_out")
        h, a_ffn = _resid_norm(h, mix, gains[layer, 1], gains[layer, 2], BF16)
        (act,) = _matmul(a_ffn, ffn_w_gate_up[layer].astype(BF16), (0, d_ff), d_ff, _ep_swiglu, (BF16,),
                         name="ffn_up")
        (f_out,) = _matmul(act, ffn_w_down[layer].astype(BF16), (0,), d, _ep_identity, (F32,), bm=512,
                           name="ffn_down")
        if layer + 1 < depth:
            a_dtype = F32 if (layer + 1) % 2 == 0 else BF16
            h, a = _resid_norm(h, f_out, gains[layer, 3], gains[layer + 1, 0], a_dtype)
        else:
            h, _ = _resid_norm(h, f_out, gains[layer, 3], None, None)
    return h.reshape(batch, seq, d).astype(x.dtype)
```

```python
import functools

import jax
import jax.numpy as jnp
from jax import lax
from jax.experimental import pallas as pl
from jax.experimental.pallas import tpu as pltpu

F32 = jnp.float32
BF16 = jnp.bfloat16
HIGHEST = lax.Precision.HIGHEST

RMS_EPS = 1e-6
S5_GROUP = 16
S5_STATE = 64
S5_EIG_CLIP = 1e-4
S5_CHUNK = 16
S5_GROUP_BLOCK = 8
HGRN_HEAD = 128
HGRN_CHUNK = 64
HGRN_SUB = 16
HGRN_SAFE_DECAY = 60.0
LANES = 128
SUBLANES = 8
VMEM_LIMIT = 56 * 1024 * 1024


def _cparams(*sem):
    return pltpu.CompilerParams(dimension_semantics=sem, vmem_limit_bytes=VMEM_LIMIT)


def _tile(n, want):
    t = min(n, want)
    assert n % t == 0, (n, want)
    return t


def _lower_bounds_kernel(logit_ref, out_ref):
    x = logit_ref[...]
    e = jnp.exp(x - jnp.max(x, axis=0, keepdims=True))
    p = e / jnp.sum(e, axis=0, keepdims=True)
    acc = jnp.zeros_like(p[0:1])
    rows = [acc]
    for layer in range(1, x.shape[0]):
        acc = acc + p[layer:layer + 1]
        rows.append(acc)
    out_ref[...] = jnp.concatenate(rows, axis=0)


def _lower_bounds(logits):
    return pl.pallas_call(
        _lower_bounds_kernel,
        out_shape=jax.ShapeDtypeStruct(logits.shape, F32),
        name="lower_bounds",
    )(logits.astype(F32))


def _rms(x, gain):
    return x * lax.rsqrt(jnp.mean(x * x, axis=-1, keepdims=True) + RMS_EPS) * gain


def _norm_kernel(x_ref, g_ref, o_ref):
    o_ref[...] = _rms(x_ref[...], g_ref[...]).astype(o_ref.dtype)


def _norm(x, gain, out_dtype, bm=256):
    m, d = x.shape
    bm = _tile(m, bm)
    row = pl.BlockSpec((bm, d), lambda i: (i, 0))
    vec = pl.BlockSpec((1, d), lambda i: (0, 0))
    return pl.pallas_call(
        _norm_kernel, grid=(m // bm,), in_specs=[row, vec], out_specs=row,
        out_shape=jax.ShapeDtypeStruct((m, d), out_dtype),
        compiler_params=_cparams("parallel"), name="rms_norm",
    )(x, gain.reshape(1, d))


def _resid_norm_kernel(h_ref, m_ref, gpost_ref, gpre_ref, h_out_ref, a_out_ref):
    h = h_ref[...] + _rms(m_ref[...], gpost_ref[...])
    h_out_ref[...] = h
    a_out_ref[...] = _rms(h, gpre_ref[...]).astype(a_out_ref.dtype)


def _resid_kernel(h_ref, m_ref, gpost_ref, h_out_ref):
    h_out_ref[...] = h_ref[...] + _rms(m_ref[...], gpost_ref[...])


def _resid_norm(h, m_out, g_post, g_pre, a_dtype, bm=256):
    m, d = h.shape
    bm = _tile(m, bm)
    row = pl.BlockSpec((bm, d), lambda i: (i, 0))
    vec = pl.BlockSpec((1, d), lambda i: (0, 0))
    if g_pre is None:
        return pl.pallas_call(
            _resid_kernel, grid=(m // bm,), in_specs=[row, row, vec], out_specs=row,
            out_shape=jax.ShapeDtypeStruct((m, d), F32),
            compiler_params=_cparams("parallel"), name="resid",
        )(h, m_out, g_post.reshape(1, d)), None
    return pl.pallas_call(
        _resid_norm_kernel, grid=(m // bm,), in_specs=[row, row, vec, vec],
        out_specs=[row, row],
        out_shape=[jax.ShapeDtypeStruct((m, d), F32), jax.ShapeDtypeStruct((m, d), a_dtype)],
        compiler_params=_cparams("parallel"), name="resid_norm",
    )(h, m_out, g_post.reshape(1, d), g_pre.reshape(1, d))


def _mm_kernel(*refs, n_w, n_vec, epilogue):
    x_ref = refs[0]
    w_refs = refs[1:1 + n_w]
    vec_refs = refs[1 + n_w:1 + n_w + n_vec]
    out_refs = refs[1 + n_w + n_vec:]
    x = x_ref[...]
    accs = [jnp.dot(x, w[...].astype(BF16), preferred_element_type=F32) for w in w_refs]
    outs = epilogue(*accs, *[v[...] for v in vec_refs])
    for o_ref, val in zip(out_refs, outs):
        o_ref[...] = val.astype(o_ref.dtype)


def _matmul(x, w, layer, col_starts, n_cols, epilogue, out_dtypes, vecs=(), bm=1024, bn=256, x_buffers=2,
            name="matmul"):
    m, kdim = x.shape
    bm = _tile(m, bm)
    bn = _tile(n_cols, bn)
    x_spec = pl.BlockSpec((bm, kdim), lambda i, j: (i, 0), pipeline_mode=pl.Buffered(x_buffers))
    w_specs = []
    for s in col_starts:
        assert s % bn == 0
        w_specs.append(pl.BlockSpec((None, kdim, bn),
                                    functools.partial(lambda i, j, off: (layer, 0, j + off), off=s // bn)))
    vec_spec = pl.BlockSpec((1, bn), lambda i, j: (0, j))
    out_spec = pl.BlockSpec((bm, bn), lambda i, j: (i, j))
    outs = pl.pallas_call(
        functools.partial(_mm_kernel, n_w=len(col_starts), n_vec=len(vecs), epilogue=epilogue),
        grid=(m // bm, n_cols // bn),
        in_specs=[x_spec] + w_specs + [vec_spec] * len(vecs),
        out_specs=[out_spec] * len(out_dtypes),
        out_shape=[jax.ShapeDtypeStruct((m, n_cols), dt) for dt in out_dtypes],
        compiler_params=_cparams("parallel", "arbitrary"), name=name,
    )(x, *([w] * len(col_starts)), *[v.reshape(1, n_cols) for v in vecs])
    return outs


def _ep_identity(acc):
    return (acc,)


def _ep_silu(acc):
    return (acc * jax.nn.sigmoid(acc),)


def _ep_glu(val, gate):
    return (val * jax.nn.sigmoid(gate),)


def _ep_swiglu(gate, up):
    return (gate * jax.nn.sigmoid(gate) * up,)


def _ep_forget(f, lb):
    log_sig = jnp.minimum(f, 0.0) - jnp.log1p(jnp.exp(-jnp.abs(f)))
    a = jnp.log(lb)
    b = jnp.log1p(-lb) + log_sig
    log_forget = jnp.maximum(a, b) + jnp.log1p(jnp.exp(-jnp.abs(a - b)))
    key = (1.0 - lb) * jax.nn.sigmoid(-f)
    return log_forget, key


def _s5_prep_kernel(are_ref, aim_ref, ldt_ref, btr_ref, bti_ref, cre_ref, cim_ref,
                    mw_ref, o_ref, al_ref):
    gb = are_ref.shape[0]
    h, p, big_l = S5_GROUP, S5_STATE, S5_CHUNK
    lane = lax.broadcasted_iota(jnp.int32, (h, big_l * h), 1)
    zpad = jnp.zeros((h, LANES - p), F32)
    for g in range(gb):
        lam_re = jnp.minimum(are_ref[g], -S5_EIG_CLIP)
        lam_im = aim_ref[g]
        dt = jnp.exp(ldt_ref[g])
        mag = jnp.exp(lam_re * dt)
        ar = mag * jnp.cos(lam_im * dt)
        ai = mag * jnp.sin(lam_im * dt)
        den = lam_re * lam_re + lam_im * lam_im
        z_re = ((ar - 1.0) * lam_re + ai * lam_im) / den
        z_im = (ai * lam_re - (ar - 1.0) * lam_im) / den
        btr, bti = btr_ref[g], bti_ref[g]
        bbr = z_re * btr - z_im * bti
        bbi = z_re * bti + z_im * btr
        cre, cim = cre_ref[g], cim_ref[g]
        pr = jnp.ones_like(ar)
        pi = jnp.zeros_like(ar)
        cp_rows = []
        for tau in range(big_l + 1):
            car = cre * pr - cim * pi
            cai = cre * pi + cim * pr
            if tau < big_l:
                cp_rows.append(jnp.concatenate([car, -cai], axis=-1))
                bar = bbr * pr - bbi * pi
                bai = bbr * pi + bbi * pr
                s = big_l - 1 - tau
                mw_ref[g, s * h:(s + 1) * h, 256:384] = jnp.concatenate([bar, zpad], -1).astype(mw_ref.dtype)
                mw_ref[g, s * h:(s + 1) * h, 384:512] = jnp.concatenate([bai, zpad], -1).astype(mw_ref.dtype)
            if tau >= 1:
                t = tau - 1
                o_ref[g, 0, t * h:(t + 1) * h, :] = jnp.concatenate([car, zpad], -1)
                o_ref[g, 1, t * h:(t + 1) * h, :] = jnp.concatenate([-cai, zpad], -1)
            if tau == big_l:
                al_ref[g, 0:1, :] = jnp.concatenate([pr, zpad[0:1]], -1)
                al_ref[g, 1:2, :] = jnp.concatenate([pi, zpad[0:1]], -1)
            pr, pi = pr * ar - pi * ai, pr * ai + pi * ar
        cp = jnp.concatenate(cp_rows, axis=0)
        bb = jnp.concatenate([bbr, bbi], axis=-1)
        krow = lax.dot_general(bb, cp, (((1,), (1,)), ((), ())), precision=HIGHEST,
                               preferred_element_type=F32)
        for s in range(big_l):
            shifted = krow if s == 0 else pltpu.roll(krow, s * h, axis=1)
            mw_ref[g, s * h:(s + 1) * h, 0:256] = jnp.where(lane >= s * h, shifted, 0.0).astype(mw_ref.dtype)


def _s5_prep(a_re, a_im, log_dt, b_re, b_im, c_re, c_im):
    g, p = a_re.shape
    h, big_l = S5_GROUP, S5_CHUNK
    gb = _tile(g, S5_GROUP_BLOCK)
    row = lambda z: z.astype(F32).reshape(g, 1, p)
    ldt = jnp.broadcast_to(log_dt.astype(F32)[:, None, None], (g, 1, p))
    bt = lambda z: jnp.swapaxes(z.astype(F32), 1, 2)
    vspec = pl.BlockSpec((gb, 1, p), lambda i: (i, 0, 0))
    mspec = pl.BlockSpec((gb, h, p), lambda i: (i, 0, 0))
    return pl.pallas_call(
        _s5_prep_kernel, grid=(g // gb,),
        in_specs=[vspec, vspec, vspec, mspec, mspec, mspec, mspec],
        out_specs=[pl.BlockSpec((gb, big_l * h, 512), lambda i: (i, 0, 0)),
                   pl.BlockSpec((gb, 2, big_l * h, LANES), lambda i: (i, 0, 0, 0)),
                   pl.BlockSpec((gb, 2, LANES), lambda i: (i, 0, 0))],
        out_shape=[jax.ShapeDtypeStruct((g, big_l * h, 512), BF16),
                   jax.ShapeDtypeStruct((g, 2, big_l * h, LANES), F32),
                   jax.ShapeDtypeStruct((g, 2, LANES), F32)],
        compiler_params=_cparams("parallel"), name="s5_prep",
    )(row(a_re), row(a_im), ldt, bt(b_re), bt(b_im), c_re.astype(F32), c_im.astype(F32))


def _s5_main_kernel(u_ref, mw_ref, o_ref, al_ref, y_ref, wre_ref, wim_ref, sre_ref, sim_ref,
                    *, batch, n_chunks):
    gb = u_ref.shape[0]
    nseq = gb * batch
    for g in range(gb):
        r = jnp.dot(u_ref[g], mw_ref[g], preferred_element_type=F32)
        y_ref[g] = r[:, 0:256]
        for b in range(batch):
            rows = slice(b * n_chunks, (b + 1) * n_chunks)
            wre_ref[pl.ds(g * batch + b, n_chunks, stride=nseq), :] = r[rows, 256:384]
            wim_ref[pl.ds(g * batch + b, n_chunks, stride=nseq), :] = r[rows, 384:512]
    ar = jnp.concatenate([al_ref[g, 0:1, :] for g in range(gb) for _ in range(batch)], axis=0)
    ai = jnp.concatenate([al_ref[g, 1:2, :] for g in range(gb) for _ in range(batch)], axis=0)

    def step(c, carry):
        s_re, s_im = carry
        base = pl.multiple_of(c * nseq, nseq)
        sre_ref[pl.ds(base, nseq), :] = s_re
        sim_ref[pl.ds(base, nseq), :] = s_im
        n_re = ar * s_re - ai * s_im + wre_ref[pl.ds(base, nseq), :]
        n_im = ar * s_im + ai * s_re + wim_ref[pl.ds(base, nseq), :]
        return n_re, n_im

    zero = jnp.zeros((nseq, LANES), F32)
    lax.fori_loop(0, n_chunks, step, (zero, zero))
    nt = (((1,), (1,)), ((), ()))
    for g in range(gb):
        for b in range(batch):
            s_re = sre_ref[pl.ds(g * batch + b, n_chunks, stride=nseq), :]
            s_im = sim_ref[pl.ds(g * batch + b, n_chunks, stride=nseq), :]
            ys = (lax.dot_general(s_re, o_ref[g, 0], nt, precision=HIGHEST, preferred_element_type=F32)
                  + lax.dot_general(s_im, o_ref[g, 1], nt, precision=HIGHEST, preferred_element_type=F32))
            y_ref[g, b * n_chunks:(b + 1) * n_chunks, :] += ys


def _s5_core(a, batch, mw, o_mat, al):
    m, d = a.shape
    t = m // batch
    h, big_l = S5_GROUP, S5_CHUNK
    g = d // h
    n_chunks = t // big_l
    gb = _tile(g, S5_GROUP_BLOCK)
    u = (a.reshape(batch, n_chunks, big_l, g, h).transpose(3, 0, 1, 2, 4)
         .reshape(g, batch * n_chunks, big_l * h).astype(BF16))
    nrow = n_chunks * gb * batch
    y = pl.pallas_call(
        functools.partial(_s5_main_kernel, batch=batch, n_chunks=n_chunks),
        grid=(g // gb,),
        in_specs=[pl.BlockSpec((gb, batch * n_chunks, big_l * h), lambda i: (i, 0, 0)),
                  pl.BlockSpec((gb, big_l * h, 512), lambda i: (i, 0, 0)),
                  pl.BlockSpec((gb, 2, big_l * h, LANES), lambda i: (i, 0, 0, 0)),
                  pl.BlockSpec((gb, 2, LANES), lambda i: (i, 0, 0))],
        out_specs=pl.BlockSpec((gb, batch * n_chunks, big_l * h), lambda i: (i, 0, 0)),
        out_shape=jax.ShapeDtypeStruct((g, batch * n_chunks, big_l * h), F32),
        scratch_shapes=[pltpu.VMEM((nrow, LANES), F32)] * 4,
        compiler_params=_cparams("parallel"), name="s5_main",
    )(u, mw, o_mat, al)
    return (y.reshape(g, batch, n_chunks, big_l, h).transpose(1, 2, 3, 0, 4).reshape(m, d))


def _s5_act_kernel(y_ref, a_ref, d_ref, z_ref):
    z_ref[...] = jax.nn.gelu(y_ref[...] + d_ref[...] * a_ref[...]).astype(z_ref.dtype)


def _s5_act(y, a, d_skip, bm=256):
    m, d = y.shape
    bm = _tile(m, bm)
    row = pl.BlockSpec((bm, d), lambda i: (i, 0))
    vec = pl.BlockSpec((1, d), lambda i: (0, 0))
    return pl.pallas_call(
        _s5_act_kernel, grid=(m // bm,), in_specs=[row, row, vec], out_specs=row,
        out_shape=jax.ShapeDtypeStruct((m, d), BF16),
        compiler_params=_cparams("parallel"), name="s5_act",
    )(y, a, d_skip.astype(F32).reshape(1, d))


def _hgrn_stage1(q, k, cum, crow, vb, st, diag_mask, with_diag):
    nt = (((1,), (1,)), ((), ()))
    tn = (((0,), (0,)), ((), ()))
    r15, r31, r47, r63 = crow(15), crow(31), crow(47), crow(63)

    def zrows(n):
        return jnp.zeros((n, HGRN_HEAD), F32)

    q0 = q * jnp.exp(cum)
    q15 = q[16:32] * jnp.exp(cum[16:32] - r15)
    q31 = q[32:64] * jnp.exp(cum[32:64] - r31)
    q47 = q[48:64] * jnp.exp(cum[48:64] - r47)
    k63 = k * jnp.exp(r63 - cum)
    k31 = k[0:32] * jnp.exp(r31 - cum[0:32])
    k15 = k[0:16] * jnp.exp(r15 - cum[0:16])
    k47 = k[32:48] * jnp.exp(r47 - cum[32:48])
    o_state = lax.dot_general(q0.astype(BF16), st.astype(BF16), nt, preferred_element_type=F32)
    st_new = st * jnp.exp(r63) + lax.dot_general(vb, k63.astype(BF16), tn, preferred_element_type=F32)
    q_cat = jnp.concatenate([
        jnp.concatenate([zrows(32), q31], axis=0),
        jnp.concatenate([zrows(16), q15, zrows(32)], axis=0),
        jnp.concatenate([zrows(48), q47], axis=0)], axis=-1).astype(BF16)
    k_cat = jnp.concatenate([
        jnp.concatenate([k31, zrows(32)], axis=0),
        jnp.concatenate([k15, zrows(48)], axis=0),
        jnp.concatenate([zrows(32), k47, zrows(16)], axis=0)], axis=-1).astype(BF16)
    scores = lax.dot_general(q_cat, k_cat, nt, preferred_element_type=F32)
    if with_diag:
        q_d = jnp.concatenate([q0[0:16], q15, q31[0:16], q47], axis=0).astype(BF16)
        k_d = jnp.concatenate([
            k[0:16] * jnp.exp(jnp.minimum(-cum[0:16], HGRN_SAFE_DECAY)),
            k[16:32] * jnp.exp(jnp.minimum(r15 - cum[16:32], HGRN_SAFE_DECAY)),
            k[32:48] * jnp.exp(jnp.minimum(r31 - cum[32:48], HGRN_SAFE_DECAY)),
            k[48:64] * jnp.exp(jnp.minimum(r47 - cum[48:64], HGRN_SAFE_DECAY))], axis=0).astype(BF16)
        scores = scores + jnp.where(diag_mask, lax.dot_general(q_d, k_d, nt, preferred_element_type=F32), 0.0)
    return o_state, st_new, scores


def _hgrn_exact_diag(qtile, krow, crow):
    c_len, sub = HGRN_CHUNK, HGRN_SUB
    row8 = lax.broadcasted_iota(jnp.int32, (SUBLANES, HGRN_HEAD), 0)
    lane8 = lax.broadcasted_iota(jnp.int32, (SUBLANES, HGRN_HEAD), 1)
    pairs = [(t0, list(range((t0 // sub) * sub, t0 + SUBLANES))) for t0 in range(0, c_len, SUBLANES)]
    e_tiles = []
    for t0, s_list in pairs:
        qt = qtile(t0)
        ct = jnp.concatenate([crow(t0 + i) for i in range(SUBLANES)], axis=0)
        for s in s_list:
            e = qt * krow(s) * jnp.exp(ct - crow(s))
            if s >= t0:
                e = jnp.where(row8 >= s - t0, e, 0.0)
            e_tiles.append(e)
    e_all = jnp.concatenate(e_tiles, axis=0).astype(BF16)
    sums = jnp.dot(e_all, jnp.ones((HGRN_HEAD, HGRN_HEAD), BF16), preferred_element_type=F32)
    diag_tiles = []
    idx = 0
    for t0, s_list in pairs:
        acc = jnp.zeros((SUBLANES, HGRN_HEAD), F32)
        for s in s_list:
            acc = jnp.where(lane8 == s, sums[idx * SUBLANES:(idx + 1) * SUBLANES], acc)
            idx += 1
        diag_tiles.append(acc)
    return jnp.concatenate(diag_tiles, axis=0)[:, 0:c_len]


def _hgrn_finish(o, scores, vb, gn, sg):
    o = o + jnp.dot(scores.astype(BF16), vb, preferred_element_type=F32)
    o = o * lax.rsqrt(jnp.mean(o * o, axis=-1, keepdims=True) + RMS_EPS)
    return o * gn * sg.astype(F32)


def _hgrn_core_kernel(q_ref, k_ref, lf_ref, v_ref, sg_ref, gn_ref, o_ref, st_ref, cum_ref, kc_ref, cc_ref,
                      tri_ref, *, heads):
    c_len, sub, hd_dim = HGRN_CHUNK, HGRN_SUB, HGRN_HEAD
    tb = q_ref.shape[0]
    n_chunks = tb // c_len
    assert c_len == 4 * sub

    @pl.when(pl.program_id(2) == 0)
    def _():
        st_ref[...] = jnp.zeros_like(st_ref)
        ri = lax.broadcasted_iota(jnp.int32, (tb, tb), 0)
        ci = lax.broadcasted_iota(jnp.int32, (tb, tb), 1)
        tri = ((ri >= ci) & ((ri // c_len) == (ci // c_len))).astype(BF16)
        tri_ref[...] = jnp.concatenate([tri, tri, tri], axis=1)

    lf = lf_ref[...]
    lf_hi = lf.astype(BF16)
    rest = lf - lf_hi.astype(F32)
    lf_mid = rest.astype(BF16)
    lf_lo = (rest - lf_mid.astype(F32)).astype(BF16)
    cum_ref[...] = jnp.dot(tri_ref[...], jnp.concatenate([lf_hi, lf_mid, lf_lo], axis=0),
                           preferred_element_type=F32)
    worst = jnp.zeros((1, cum_ref.shape[1]), F32)
    for c in range(n_chunks):
        prev = worst * 0.0
        for i in range(sub - 1, c_len, sub):
            cur = cum_ref[c * c_len + i:c * c_len + i + 1, :]
            worst = jnp.maximum(worst, prev - cur)
            prev = cur
    exact_needed = jnp.max(worst) > HGRN_SAFE_DECAY

    ri = lax.broadcasted_iota(jnp.int32, (c_len, c_len), 0)
    ci = lax.broadcasted_iota(jnp.int32, (c_len, c_len), 1)
    diag_mask = (ri >= ci) & ((ri // sub) == (ci // sub))
    lanes = [slice(hd * hd_dim, (hd + 1) * hd_dim) for hd in range(heads)]

    @pl.when(jnp.logical_not(exact_needed))
    def _():
        for c in range(n_chunks):
            r0 = c * c_len
            rows = slice(r0, r0 + c_len)
            for hd, ls in enumerate(lanes):
                vb = v_ref[rows, ls].astype(BF16)
                o, st_new, scores = _hgrn_stage1(
                    q_ref[rows, ls], k_ref[rows, ls], cum_ref[rows, ls],
                    lambda i, r0=r0, ls=ls: cum_ref[r0 + i:r0 + i + 1, ls],
                    vb, st_ref[hd], diag_mask, True)
                st_ref[hd] = st_new
                o_ref[rows, ls] = _hgrn_finish(o, scores, vb, gn_ref[:, ls], sg_ref[rows, ls]).astype(o_ref.dtype)

    @pl.when(exact_needed)
    def _():
        def chunk_body(c, carry):
            r0 = pl.multiple_of(c * c_len, c_len)
            rows = pl.ds(r0, c_len)
            cc_ref[...] = cum_ref[rows, :]
            kc_ref[...] = k_ref[rows, :]
            for hd, ls in enumerate(lanes):
                vb = v_ref[rows, ls].astype(BF16)
                o, st_new, scores = _hgrn_stage1(
                    q_ref[rows, ls], kc_ref[:, ls], cc_ref[:, ls], lambda i, ls=ls: cc_ref[i:i + 1, ls],
                    vb, st_ref[hd], diag_mask, False)
                st_ref[hd] = st_new
                scores = scores + _hgrn_exact_diag(
                    lambda t0, ls=ls: q_ref[pl.ds(r0 + t0, SUBLANES), ls],
                    lambda s, ls=ls: kc_ref[s:s + 1, ls],
                    lambda s, ls=ls: cc_ref[s:s + 1, ls])
                o_ref[rows, ls] = _hgrn_finish(o, scores, vb, gn_ref[:, ls], sg_ref[rows, ls]).astype(o_ref.dtype)
            return carry

        lax.fori_loop(0, n_chunks, chunk_body, 0)


def _hgrn_core(q, k, log_f, v, sg, g_norm, batch, tb=256, heads=4):
    m, d = q.shape
    t = m // batch
    tb = _tile(t, tb)
    heads = _tile(d // HGRN_HEAD, heads)
    wl = heads * HGRN_HEAD
    nt = t // tb
    blk = pl.BlockSpec((tb, wl), lambda b, j, i: (b * nt + i, j))
    vec = pl.BlockSpec((1, wl), lambda b, j, i: (0, j))
    return pl.pallas_call(
        functools.partial(_hgrn_core_kernel, heads=heads),
        grid=(batch, d // wl, nt),
        in_specs=[blk, blk, blk, blk, blk, vec],
        out_specs=blk,
        out_shape=jax.ShapeDtypeStruct((m, d), BF16),
        scratch_shapes=[pltpu.VMEM((heads, HGRN_HEAD, HGRN_HEAD), F32),
                        pltpu.VMEM((tb, wl), F32),
                        pltpu.VMEM((HGRN_CHUNK, wl), F32),
                        pltpu.VMEM((HGRN_CHUNK, wl), F32),
                        pltpu.VMEM((tb, 3 * tb), BF16)],
        compiler_params=_cparams("parallel", "parallel", "arbitrary"), name="hgrn_core",
    )(q, k, log_f, v, sg, g_norm.astype(F32).reshape(1, d))


def kernel(x, norm_gains, s5_a_re, s5_a_im, s5_log_dt, s5_b_re, s5_b_im, s5_c_re, s5_c_im, s5_d,
           s5_w_glu, hgrn_w_in, hgrn_lb_logits, hgrn_g_norm, hgrn_w_out, ffn_w_gate_up, ffn_w_down):
    batch, seq, d = x.shape
    depth = norm_gains.shape[0]
    d_ff = ffn_w_down.shape[1]
    m = batch * seq
    gains = norm_gains.astype(F32)
    lower_bounds = _lower_bounds(hgrn_lb_logits)
    w_down = ffn_w_down.astype(BF16)

    h = x.reshape(m, d).astype(F32)
    a = _norm(h, gains[0, 0], F32)
    for layer in range(depth):
        j = layer // 2
        if layer % 2 == 0:
            mw, o_mat, al = _s5_prep(s5_a_re[j], s5_a_im[j], s5_log_dt[j], s5_b_re[j], s5_b_im[j],
                                     s5_c_re[j], s5_c_im[j])
            y = _s5_core(a, batch, mw, o_mat, al)
            z = _s5_act(y, a, s5_d[j])
            (mix,) = _matmul(z, s5_w_glu, j, (0, d), d, _ep_glu, (F32,), name="s5_glu")
        else:
            (q,) = _matmul(a, hgrn_w_in, j, (0,), d, _ep_silu, (F32,), bn=512, name="hgrn_q")
            log_f, key = _matmul(a, hgrn_w_in, j, (d,), d, _ep_forget, (F32, F32), vecs=(lower_bounds[layer],),
                                 bn=512, name="hgrn_f")
            (val,) = _matmul(a, hgrn_w_in, j, (2 * d,), d, _ep_identity, (F32,), bn=512, name="hgrn_v")
            (sg,) = _matmul(a, hgrn_w_in, j, (3 * d,), d, _ep_silu, (BF16,), bn=512, name="hgrn_g")
            o = _hgrn_core(q, key, log_f, val, sg, hgrn_g_norm[j], batch)
            (mix,) = _matmul(o, hgrn_w_out, j, (0,), d, _ep_identity, (F32,), bn=512, name="hgrn_out")
        h, a_ffn = _resid_norm(h, mix, gains[layer, 1], gains[layer, 2], BF16)
        (act,) = _matmul(a_ffn, ffn_w_gate_up, layer, (0, d_ff), d_ff, _ep_swiglu, (BF16,), name="ffn_up")
        (f_out,) = _matmul(act, w_down, layer, (0,), d, _ep_identity, (F32,), x_buffers=1, name="ffn_down")
        if layer + 1 < depth:
            a_dtype = F32 if (layer + 1) % 2 == 0 else BF16
            h, a = _resid_norm(h, f_out, gains[layer, 3], gains[layer + 1, 0], a_dtype)
        else:
            h, _ = _resid_norm(h, f_out, gains[layer, 3], None, None)
    return h.reshape(batch, seq, d).astype(x.dtype)
```

```python
import functools

import jax
import jax.numpy as jnp
from jax import lax
from jax.experimental import pallas as pl
from jax.experimental.pallas import tpu as pltpu

F32 = jnp.float32
BF16 = jnp.bfloat16
HIGHEST = lax.Precision.HIGHEST

RMS_EPS = 1e-6
S5_GROUP = 16
S5_STATE = 64
S5_EIG_CLIP = 1e-4
S5_CHUNK = 16
HGRN_HEAD = 128
HGRN_CHUNK = 64
HGRN_SUB = 16
HGRN_SAFE_DECAY = 60.0
LANES = 128
SUBLANES = 8
VMEM_LIMIT = 56 * 1024 * 1024


def _cparams(*sem):
    return pltpu.CompilerParams(dimension_semantics=sem, vmem_limit_bytes=VMEM_LIMIT)


def _tile(n, want):
    t = min(n, want)
    assert n % t == 0, (n, want)
    return t


def _lower_bounds_kernel(logit_ref, out_ref):
    x = logit_ref[...]
    e = jnp.exp(x - jnp.max(x, axis=0, keepdims=True))
    p = e / jnp.sum(e, axis=0, keepdims=True)
    acc = jnp.zeros_like(p[0:1])
    rows = [acc]
    for layer in range(1, x.shape[0]):
        acc = acc + p[layer:layer + 1]
        rows.append(acc)
    out_ref[...] = jnp.concatenate(rows, axis=0)


def _lower_bounds(logits):
    return pl.pallas_call(
        _lower_bounds_kernel,
        out_shape=jax.ShapeDtypeStruct(logits.shape, F32),
        name="lower_bounds",
    )(logits.astype(F32))


def _rms(x, gain):
    return x * lax.rsqrt(jnp.mean(x * x, axis=-1, keepdims=True) + RMS_EPS) * gain


def _norm_kernel(x_ref, g_ref, o_ref):
    o_ref[...] = _rms(x_ref[...], g_ref[...]).astype(o_ref.dtype)


def _norm(x, gain, out_dtype, bm=256):
    m, d = x.shape
    bm = _tile(m, bm)
    row = pl.BlockSpec((bm, d), lambda i: (i, 0))
    vec = pl.BlockSpec((1, d), lambda i: (0, 0))
    return pl.pallas_call(
        _norm_kernel, grid=(m // bm,), in_specs=[row, vec], out_specs=row,
        out_shape=jax.ShapeDtypeStruct((m, d), out_dtype),
        compiler_params=_cparams("parallel"), name="rms_norm",
    )(x, gain.reshape(1, d))


def _resid_norm_kernel(h_ref, m_ref, gpost_ref, gpre_ref, h_out_ref, a_out_ref):
    h = h_ref[...] + _rms(m_ref[...], gpost_ref[...])
    h_out_ref[...] = h
    a_out_ref[...] = _rms(h, gpre_ref[...]).astype(a_out_ref.dtype)


def _resid_kernel(h_ref, m_ref, gpost_ref, h_out_ref):
    h_out_ref[...] = h_ref[...] + _rms(m_ref[...], gpost_ref[...])


def _resid_norm(h, m_out, g_post, g_pre, a_dtype, bm=256):
    m, d = h.shape
    bm = _tile(m, bm)
    row = pl.BlockSpec((bm, d), lambda i: (i, 0))
    vec = pl.BlockSpec((1, d), lambda i: (0, 0))
    if g_pre is None:
        return pl.pallas_call(
            _resid_kernel, grid=(m // bm,), in_specs=[row, row, vec], out_specs=row,
            out_shape=jax.ShapeDtypeStruct((m, d), F32),
            compiler_params=_cparams("parallel"), name="resid",
        )(h, m_out, g_post.reshape(1, d)), None
    return pl.pallas_call(
        _resid_norm_kernel, grid=(m // bm,), in_specs=[row, row, vec, vec],
        out_specs=[row, row],
        out_shape=[jax.ShapeDtypeStruct((m, d), F32), jax.ShapeDtypeStruct((m, d), a_dtype)],
        compiler_params=_cparams("parallel"), name="resid_norm",
    )(h, m_out, g_post.reshape(1, d), g_pre.reshape(1, d))


def _mm_kernel(*refs, n_w, n_vec, epilogue):
    x_ref = refs[0]
    w_refs = refs[1:1 + n_w]
    vec_refs = refs[1 + n_w:1 + n_w + n_vec]
    out_refs = refs[1 + n_w + n_vec:]
    x = x_ref[...]
    accs = [jnp.dot(x, w[...].astype(BF16), preferred_element_type=F32) for w in w_refs]
    outs = epilogue(*accs, *[v[...] for v in vec_refs])
    for o_ref, val in zip(out_refs, outs):
        o_ref[...] = val.astype(o_ref.dtype)


def _matmul(x, w, layer, col_starts, n_cols, epilogue, out_dtypes, vecs=(), bm=1024, bn=256, x_buffers=2,
            name="matmul"):
    m, kdim = x.shape
    bm = _tile(m, bm)
    bn = _tile(n_cols, bn)
    x_spec = pl.BlockSpec((bm, kdim), lambda i, j: (i, 0), pipeline_mode=pl.Buffered(x_buffers))
    w_specs = []
    for s in col_starts:
        assert s % bn == 0
        w_specs.append(pl.BlockSpec((None, kdim, bn),
                                    functools.partial(lambda i, j, off: (layer, 0, j + off), off=s // bn)))
    vec_spec = pl.BlockSpec((1, bn), lambda i, j: (0, j))
    out_spec = pl.BlockSpec((bm, bn), lambda i, j: (i, j))
    outs = pl.pallas_call(
        functools.partial(_mm_kernel, n_w=len(col_starts), n_vec=len(vecs), epilogue=epilogue),
        grid=(m // bm, n_cols // bn),
        in_specs=[x_spec] + w_specs + [vec_spec] * len(vecs),
        out_specs=[out_spec] * len(out_dtypes),
        out_shape=[jax.ShapeDtypeStruct((m, n_cols), dt) for dt in out_dtypes],
        compiler_params=_cparams("parallel", "arbitrary"), name=name,
    )(x, *([w] * len(col_starts)), *[v.reshape(1, n_cols) for v in vecs])
    return outs


def _ep_identity(acc):
    return (acc,)


def _ep_silu(acc):
    return (acc * jax.nn.sigmoid(acc),)


def _ep_glu(val, gate):
    return (val * jax.nn.sigmoid(gate),)


def _ep_swiglu(gate, up):
    return (gate * jax.nn.sigmoid(gate) * up,)


def _ep_forget(f, lb):
    log_sig = jnp.minimum(f, 0.0) - jnp.log1p(jnp.exp(-jnp.abs(f)))
    a = jnp.log(lb)
    b = jnp.log1p(-lb) + log_sig
    log_forget = jnp.maximum(a, b) + jnp.log1p(jnp.exp(-jnp.abs(a - b)))
    key = (1.0 - lb) * jax.nn.sigmoid(-f)
    return log_forget, key


def _s5_prep_kernel(are_ref, aim_ref, ldt_ref, btr_ref, bti_ref, cre_ref, cim_ref,
                    dt_ref, bac_ref, cpc_ref, al_ref):
    gb = are_ref.shape[0]
    h, big_l = S5_GROUP, S5_CHUNK
    lane = lax.broadcasted_iota(jnp.int32, (h, LANES), 1)
    for g in range(gb):
        rows = slice(g * h, (g + 1) * h)
        lam_re = jnp.minimum(are_ref[g], -S5_EIG_CLIP)
        lam_im = aim_ref[g]
        dt = jnp.exp(ldt_ref[g])
        mag = jnp.exp(lam_re * dt)
        ar = mag * jnp.cos(lam_im * dt)
        ai = mag * jnp.sin(lam_im * dt)
        den = lam_re * lam_re + lam_im * lam_im
        z_re = ((ar - 1.0) * lam_re + ai * lam_im) / den
        z_im = (ai * lam_re - (ar - 1.0) * lam_im) / den
        btr, bti = btr_ref[g], bti_ref[g]
        bbr = z_re * btr - z_im * bti
        bbi = z_re * bti + z_im * btr
        cre, cim = cre_ref[g], cim_ref[g]
        pr = jnp.ones_like(ar)
        pi = jnp.zeros_like(ar)
        cp_rows = []
        for tau in range(big_l + 1):
            cp = jnp.concatenate([cre * pr - cim * pi, -(cre * pi + cim * pr)], axis=-1)
            if tau < big_l:
                cp_rows.append(cp)
                ba = jnp.concatenate([bbr * pr - bbi * pi, bbr * pi + bbi * pr], axis=-1)
                bac_ref[tau, rows, :] = ba.astype(bac_ref.dtype)
            if tau >= 1:
                cpc_ref[tau - 1, rows, :] = cp.astype(cpc_ref.dtype)
            if tau == big_l:
                al_ref[0, g:g + 1, :] = jnp.concatenate([pr, pr], axis=-1)
                al_ref[1, g:g + 1, :] = jnp.concatenate([-pi, pi], axis=-1)
            pr, pi = pr * ar - pi * ai, pr * ai + pi * ar
        cp_all = jnp.concatenate(cp_rows, axis=0)
        bb = jnp.concatenate([bbr, bbi], axis=-1)
        krow = lax.dot_general(bb, cp_all, (((1,), (1,)), ((), ())), precision=HIGHEST,
                               preferred_element_type=F32)
        own = (lane >= g * h) & (lane < (g + 1) * h)
        taus_per_vreg = LANES // h
        for tau in range(big_l):
            src = krow[:, (tau // taus_per_vreg) * LANES:(tau // taus_per_vreg + 1) * LANES]
            shift = ((g - tau % taus_per_vreg) * h) % LANES
            moved = src if shift == 0 else pltpu.roll(src, shift, axis=1)
            dt_ref[tau, rows, :] = jnp.where(own, moved, 0.0).astype(dt_ref.dtype)


def _s5_prep(a_re, a_im, log_dt, b_re, b_im, c_re, c_im):
    g, p = a_re.shape
    h, big_l = S5_GROUP, S5_CHUNK
    gb = LANES // h
    assert 2 * p == LANES and g % gb == 0
    row = lambda z: z.astype(F32).reshape(g, 1, p)
    ldt = jnp.broadcast_to(log_dt.astype(F32)[:, None, None], (g, 1, p))
    bt = lambda z: jnp.swapaxes(z.astype(F32), 1, 2)
    vspec = pl.BlockSpec((gb, 1, p), lambda i: (i, 0, 0))
    mspec = pl.BlockSpec((gb, h, p), lambda i: (i, 0, 0))
    op_spec = pl.BlockSpec((None, big_l, LANES, LANES), lambda i: (i, 0, 0, 0))
    op_shape = jax.ShapeDtypeStruct((g // gb, big_l, LANES, LANES), BF16)
    return pl.pallas_call(
        _s5_prep_kernel, grid=(g // gb,),
        in_specs=[vspec, vspec, vspec, mspec, mspec, mspec, mspec],
        out_specs=[op_spec, op_spec, op_spec, pl.BlockSpec((None, 2, gb, LANES), lambda i: (i, 0, 0, 0))],
        out_shape=[op_shape, op_shape, op_shape, jax.ShapeDtypeStruct((g // gb, 2, gb, LANES), F32)],
        compiler_params=_cparams("parallel"), name="s5_prep",
    )(row(a_re), row(a_im), ldt, bt(b_re), bt(b_im), c_re.astype(F32), c_im.astype(F32))


def _s5_main_kernel(a_ref, dt_ref, bac_ref, cpc_ref, al_ref, y_ref,
                    u2_ref, bt_ref, bdw_ref, bdo_ref, w3_ref, w3s_ref, sp3_ref, *, batch, n_chunks):
    h, big_l = S5_GROUP, S5_CHUNK
    gb = LANES // h
    nrows = batch * n_chunks
    nseq = batch * gb

    @pl.when(pl.program_id(0) == 0)
    def _():
        bt_ref[...] = jnp.zeros_like(bt_ref)
        bdw_ref[...] = jnp.zeros_like(bdw_ref)
        bdo_ref[...] = jnp.zeros_like(bdo_ref)

    for s in range(big_l):
        for t in range(s, big_l):
            bt_ref[s * LANES:(s + 1) * LANES, t * LANES:(t + 1) * LANES] = dt_ref[t - s]
        for g in range(gb):
            rows = slice(s * LANES + g * h, s * LANES + (g + 1) * h)
            cols = slice(g * LANES, (g + 1) * LANES)
            bdw_ref[rows, cols] = bac_ref[big_l - 1 - s, g * h:(g + 1) * h, :]
            bdo_ref[rows, cols] = cpc_ref[s, g * h:(g + 1) * h, :]

    for s in range(big_l):
        u2_ref[:, s * LANES:(s + 1) * LANES] = a_ref[pl.ds(s, nrows, stride=big_l), :].astype(BF16)
    u2 = u2_ref[...]

    win = jnp.dot(u2, bdw_ref[...], preferred_element_type=F32)
    for b in range(batch):
        for g in range(gb):
            w3_ref[pl.ds(b * gb + g, n_chunks, stride=nseq), :] = (
                win[b * n_chunks:(b + 1) * n_chunks, g * LANES:(g + 1) * LANES])
    w3s_ref[...] = pltpu.roll(w3_ref[...], LANES // 2, axis=1)
    a1 = jnp.concatenate([al_ref[0]] * batch, axis=0)
    a2 = jnp.concatenate([al_ref[1]] * batch, axis=0)

    def step(c, carry):
        st, st_sw = carry
        base = pl.multiple_of(c * nseq, nseq)
        sp3_ref[pl.ds(base, nseq), :] = st
        new = a1 * st + a2 * st_sw + w3_ref[pl.ds(base, nseq), :]
        new_sw = a1 * st_sw - a2 * st + w3s_ref[pl.ds(base, nseq), :]
        return new, new_sw

    zero = jnp.zeros((nseq, LANES), F32)
    lax.fori_loop(0, n_chunks, step, (zero, zero))
    sprev = jnp.concatenate(
        [jnp.concatenate([sp3_ref[pl.ds(b * gb + g, n_chunks, stride=nseq), :] for b in range(batch)], axis=0)
         for g in range(gb)], axis=1).astype(BF16)

    nt = (((1,), (1,)), ((), ()))
    pair = 2 * LANES
    for tp in range(big_l * LANES // pair):
        cols = slice(tp * pair, (tp + 1) * pair)
        y2 = (jnp.dot(u2, bt_ref[:, cols], preferred_element_type=F32)
              + lax.dot_general(sprev, bdo_ref[cols, :], nt, preferred_element_type=F32))
        y_ref[pl.ds(2 * tp, nrows, stride=big_l), :] = y2[:, 0:LANES]
        y_ref[pl.ds(2 * tp + 1, nrows, stride=big_l), :] = y2[:, LANES:pair]


def _s5_core(a, batch, dt_op, bac, cpc, al):
    m, d = a.shape
    t = m // batch
    big_l = S5_CHUNK
    gb = LANES // S5_GROUP
    n_chunks = t // big_l
    nrows = batch * n_chunks
    width = big_l * LANES
    tok = pl.BlockSpec((m, LANES), lambda i: (0, i))
    op_spec = pl.BlockSpec((None, big_l, LANES, LANES), lambda i: (i, 0, 0, 0))
    return pl.pallas_call(
        functools.partial(_s5_main_kernel, batch=batch, n_chunks=n_chunks),
        grid=(d // LANES,),
        in_specs=[tok, op_spec, op_spec, op_spec, pl.BlockSpec((None, 2, gb, LANES), lambda i: (i, 0, 0, 0))],
        out_specs=tok,
        out_shape=jax.ShapeDtypeStruct((m, d), F32),
        scratch_shapes=[pltpu.VMEM((nrows, width), BF16),
                        pltpu.VMEM((width, width), BF16),
                        pltpu.VMEM((width, gb * LANES), BF16),
                        pltpu.VMEM((width, gb * LANES), BF16),
                        pltpu.VMEM((n_chunks * batch * gb, LANES), F32),
                        pltpu.VMEM((n_chunks * batch * gb, LANES), F32),
                        pltpu.VMEM((n_chunks * batch * gb, LANES), F32)],
        compiler_params=_cparams("arbitrary"), name="s5_main",
    )(a, dt_op, bac, cpc, al)


def _s5_act_kernel(y_ref, a_ref, d_ref, z_ref):
    z_ref[...] = jax.nn.gelu(y_ref[...] + d_ref[...] * a_ref[...]).astype(z_ref.dtype)


def _s5_act(y, a, d_skip, bm=256):
    m, d = y.shape
    bm = _tile(m, bm)
    row = pl.BlockSpec((bm, d), lambda i: (i, 0))
    vec = pl.BlockSpec((1, d), lambda i: (0, 0))
    return pl.pallas_call(
        _s5_act_kernel, grid=(m // bm,), in_specs=[row, row, vec], out_specs=row,
        out_shape=jax.ShapeDtypeStruct((m, d), BF16),
        compiler_params=_cparams("parallel"), name="s5_act",
    )(y, a, d_skip.astype(F32).reshape(1, d))


def _hgrn_stage1(q, k, cum, crow, vb, st, diag_mask, with_diag):
    nt = (((1,), (1,)), ((), ()))
    tn = (((0,), (0,)), ((), ()))
    r15, r31, r47, r63 = crow(15), crow(31), crow(47), crow(63)

    def zrows(n):
        return jnp.zeros((n, HGRN_HEAD), F32)

    q0 = q * jnp.exp(cum)
    q15 = q[16:32] * jnp.exp(cum[16:32] - r15)
    q31 = q[32:64] * jnp.exp(cum[32:64] - r31)
    q47 = q[48:64] * jnp.exp(cum[48:64] - r47)
    k63 = k * jnp.exp(r63 - cum)
    k31 = k[0:32] * jnp.exp(r31 - cum[0:32])
    k15 = k[0:16] * jnp.exp(r15 - cum[0:16])
    k47 = k[32:48] * jnp.exp(r47 - cum[32:48])
    o_state = lax.dot_general(q0.astype(BF16), st.astype(BF16), nt, preferred_element_type=F32)
    st_new = st * jnp.exp(r63) + lax.dot_general(vb, k63.astype(BF16), tn, preferred_element_type=F32)
    q_cat = jnp.concatenate([
        jnp.concatenate([zrows(32), q31], axis=0),
        jnp.concatenate([zrows(16), q15, zrows(32)], axis=0),
        jnp.concatenate([zrows(48), q47], axis=0)], axis=-1).astype(BF16)
    k_cat = jnp.concatenate([
        jnp.concatenate([k31, zrows(32)], axis=0),
        jnp.concatenate([k15, zrows(48)], axis=0),
        jnp.concatenate([zrows(32), k47, zrows(16)], axis=0)], axis=-1).astype(BF16)
    scores = lax.dot_general(q_cat, k_cat, nt, preferred_element_type=F32)
    if with_diag:
        q_d = jnp.concatenate([q0[0:16], q15, q31[0:16], q47], axis=0).astype(BF16)
        k_d = jnp.concatenate([
            k[0:16] * jnp.exp(jnp.minimum(-cum[0:16], HGRN_SAFE_DECAY)),
            k[16:32] * jnp.exp(jnp.minimum(r15 - cum[16:32], HGRN_SAFE_DECAY)),
            k[32:48] * jnp.exp(jnp.minimum(r31 - cum[32:48], HGRN_SAFE_DECAY)),
            k[48:64] * jnp.exp(jnp.minimum(r47 - cum[48:64], HGRN_SAFE_DECAY))], axis=0).astype(BF16)
        scores = scores + jnp.where(diag_mask, lax.dot_general(q_d, k_d, nt, preferred_element_type=F32), 0.0)
    return o_state, st_new, scores


def _hgrn_exact_diag(qtile, krow, crow):
    c_len, sub = HGRN_CHUNK, HGRN_SUB
    row8 = lax.broadcasted_iota(jnp.int32, (SUBLANES, HGRN_HEAD), 0)
    lane8 = lax.broadcasted_iota(jnp.int32, (SUBLANES, HGRN_HEAD), 1)
    pairs = [(t0, list(range((t0 // sub) * sub, t0 + SUBLANES))) for t0 in range(0, c_len, SUBLANES)]
    e_tiles = []
    for t0, s_list in pairs:
        qt = qtile(t0)
        ct = jnp.concatenate([crow(t0 + i) for i in range(SUBLANES)], axis=0)
        for s in s_list:
            e = qt * krow(s) * jnp.exp(ct - crow(s))
            if s >= t0:
                e = jnp.where(row8 >= s - t0, e, 0.0)
            e_tiles.append(e)
    e_all = jnp.concatenate(e_tiles, axis=0).astype(BF16)
    sums = jnp.dot(e_all, jnp.ones((HGRN_HEAD, HGRN_HEAD), BF16), preferred_element_type=F32)
    diag_tiles = []
    idx = 0
    for t0, s_list in pairs:
        acc = jnp.zeros((SUBLANES, HGRN_HEAD), F32)
        for s in s_list:
            acc = jnp.where(lane8 == s, sums[idx * SUBLANES:(idx + 1) * SUBLANES], acc)
            idx += 1
        diag_tiles.append(acc)
    return jnp.concatenate(diag_tiles, axis=0)[:, 0:c_len]


def _hgrn_finish(o, scores, vb, gn, sg):
    o = o + jnp.dot(scores.astype(BF16), vb, preferred_element_type=F32)
    o = o * lax.rsqrt(jnp.mean(o * o, axis=-1, keepdims=True) + RMS_EPS)
    return o * gn * sg.astype(F32)


def _hgrn_core_kernel(q_ref, k_ref, lf_ref, v_ref, sg_ref, gn_ref, o_ref, st_ref, cum_ref, kc_ref, cc_ref,
                      tri_ref, *, heads):
    c_len, sub, hd_dim = HGRN_CHUNK, HGRN_SUB, HGRN_HEAD
    tb = q_ref.shape[0]
    n_chunks = tb // c_len
    assert c_len == 4 * sub

    @pl.when(pl.program_id(2) == 0)
    def _():
        st_ref[...] = jnp.zeros_like(st_ref)
        ri = lax.broadcasted_iota(jnp.int32, (tb, tb), 0)
        ci = lax.broadcasted_iota(jnp.int32, (tb, tb), 1)
        tri = ((ri >= ci) & ((ri // c_len) == (ci // c_len))).astype(BF16)
        tri_ref[...] = jnp.concatenate([tri, tri, tri], axis=1)

    lf = lf_ref[...]
    lf_hi = lf.astype(BF16)
    rest = lf - lf_hi.astype(F32)
    lf_mid = rest.astype(BF16)
    lf_lo = (rest - lf_mid.astype(F32)).astype(BF16)
    cum_ref[...] = jnp.dot(tri_ref[...], jnp.concatenate([lf_hi, lf_mid, lf_lo], axis=0),
                           preferred_element_type=F32)
    worst = jnp.zeros((1, cum_ref.shape[1]), F32)
    for c in range(n_chunks):
        prev = worst * 0.0
        for i in range(sub - 1, c_len, sub):
            cur = cum_ref[c * c_len + i:c * c_len + i + 1, :]
            worst = jnp.maximum(worst, prev - cur)
            prev = cur
    exact_needed = jnp.max(worst) > HGRN_SAFE_DECAY

    ri = lax.broadcasted_iota(jnp.int32, (c_len, c_len), 0)
    ci = lax.broadcasted_iota(jnp.int32, (c_len, c_len), 1)
    diag_mask = (ri >= ci) & ((ri // sub) == (ci // sub))
    lanes = [slice(hd * hd_dim, (hd + 1) * hd_dim) for hd in range(heads)]

    @pl.when(jnp.logical_not(exact_needed))
    def _():
        for c in range(n_chunks):
            r0 = c * c_len
            rows = slice(r0, r0 + c_len)
            for hd, ls in enumerate(lanes):
                vb = v_ref[rows, ls].astype(BF16)
                o, st_new, scores = _hgrn_stage1(
                    q_ref[rows, ls], k_ref[rows, ls], cum_ref[rows, ls],
                    lambda i, r0=r0, ls=ls: cum_ref[r0 + i:r0 + i + 1, ls],
                    vb, st_ref[hd], diag_mask, True)
                st_ref[hd] = st_new
                o_ref[rows, ls] = _hgrn_finish(o, scores, vb, gn_ref[:, ls], sg_ref[rows, ls]).astype(o_ref.dtype)

    @pl.when(exact_needed)
    def _():
        def chunk_body(c, carry):
            r0 = pl.multiple_of(c * c_len, c_len)
            rows = pl.ds(r0, c_len)
            cc_ref[...] = cum_ref[rows, :]
            kc_ref[...] = k_ref[rows, :]
            for hd, ls in enumerate(lanes):
                vb = v_ref[rows, ls].astype(BF16)
                o, st_new, scores = _hgrn_stage1(
                    q_ref[rows, ls], kc_ref[:, ls], cc_ref[:, ls], lambda i, ls=ls: cc_ref[i:i + 1, ls],
                    vb, st_ref[hd], diag_mask, False)
                st_ref[hd] = st_new
                scores = scores + _hgrn_exact_diag(
                    lambda t0, ls=ls: q_ref[pl.ds(r0 + t0, SUBLANES), ls],
                    lambda s, ls=ls: kc_ref[s:s + 1, ls],
                    lambda s, ls=ls: cc_ref[s:s + 1, ls])
                o_ref[rows, ls] = _hgrn_finish(o, scores, vb, gn_ref[:, ls], sg_ref[rows, ls]).astype(o_ref.dtype)
            return carry

        lax.fori_loop(0, n_chunks, chunk_body, 0)


def _hgrn_core(q, k, log_f, v, sg, g_norm, batch, tb=256, heads=4):
    m, d = q.shape
    t = m // batch
    tb = _tile(t, tb)
    heads = _tile(d // HGRN_HEAD, heads)
    wl = heads * HGRN_HEAD
    nt = t // tb
    blk = pl.BlockSpec((tb, wl), lambda b, j, i: (b * nt + i, j))
    vec = pl.BlockSpec((1, wl), lambda b, j, i: (0, j))
    return pl.pallas_call(
        functools.partial(_hgrn_core_kernel, heads=heads),
        grid=(batch, d // wl, nt),
        in_specs=[blk, blk, blk, blk, blk, vec],
        out_specs=blk,
        out_shape=jax.ShapeDtypeStruct((m, d), BF16),
        scratch_shapes=[pltpu.VMEM((heads, HGRN_HEAD, HGRN_HEAD), F32),
                        pltpu.VMEM((tb, wl), F32),
                        pltpu.VMEM((HGRN_CHUNK, wl), F32),
                        pltpu.VMEM((HGRN_CHUNK, wl), F32),
                        pltpu.VMEM((tb, 3 * tb), BF16)],
        compiler_params=_cparams("parallel", "parallel", "arbitrary"), name="hgrn_core",
    )(q, k, log_f, v, sg, g_norm.astype(F32).reshape(1, d))


def kernel(x, norm_gains, s5_a_re, s5_a_im, s5_log_dt, s5_b_re, s5_b_im, s5_c_re, s5_c_im, s5_d,
           s5_w_glu, hgrn_w_in, hgrn_lb_logits, hgrn_g_norm, hgrn_w_out, ffn_w_gate_up, ffn_w_down):
    batch, seq, d = x.shape
    depth = norm_gains.shape[0]
    d_ff = ffn_w_down.shape[1]
    m = batch * seq
    gains = norm_gains.astype(F32)
    lower_bounds = _lower_bounds(hgrn_lb_logits)
    w_down = ffn_w_down.astype(BF16)

    h = x.reshape(m, d).astype(F32)
    a = _norm(h, gains[0, 0], F32)
    for layer in range(depth):
        j = layer // 2
        if layer % 2 == 0:
            s5_ops = _s5_prep(s5_a_re[j], s5_a_im[j], s5_log_dt[j], s5_b_re[j], s5_b_im[j],
                              s5_c_re[j], s5_c_im[j])
            y = _s5_core(a, batch, *s5_ops)
            z = _s5_act(y, a, s5_d[j])
            (mix,) = _matmul(z, s5_w_glu, j, (0, d), d, _ep_glu, (F32,), name="s5_glu")
        else:
            (q,) = _matmul(a, hgrn_w_in, j, (0,), d, _ep_silu, (F32,), bn=512, name="hgrn_q")
            log_f, key = _matmul(a, hgrn_w_in, j, (d,), d, _ep_forget, (F32, F32), vecs=(lower_bounds[layer],),
                                 bn=512, name="hgrn_f")
            (val,) = _matmul(a, hgrn_w_in, j, (2 * d,), d, _ep_identity, (F32,), bn=512, name="hgrn_v")
            (sg,) = _matmul(a, hgrn_w_in, j, (3 * d,), d, _ep_silu, (BF16,), bn=512, name="hgrn_g")
            o = _hgrn_core(q, key, log_f, val, sg, hgrn_g_norm[j], batch)
            (mix,) = _matmul(o, hgrn_w_out, j, (0,), d, _ep_identity, (F32,), bn=512, name="hgrn_out")
        h, a_ffn = _resid_norm(h, mix, gains[layer, 1], gains[layer, 2], BF16)
        (act,) = _matmul(a_ffn, ffn_w_gate_up, layer, (0, d_ff), d_ff, _ep_swiglu, (BF16,), name="ffn_up")
        (f_out,) = _matmul(act, w_down, layer, (0,), d, _ep_identity, (F32,), x_buffers=1, name="ffn_down")
        if layer + 1 < depth:
            a_dtype = F32 if (layer + 1) % 2 == 0 else BF16
            h, a = _resid_norm(h, f_out, gains[layer, 3], gains[layer + 1, 0], a_dtype)
        else:
            h, _ = _resid_norm(h, f_out, gains[layer, 3], None, None)
    return h.reshape(batch, seq, d).astype(x.dtype)
```

```python
import functools

import jax
import jax.numpy as jnp
from jax import lax
from jax.experimental import pallas as pl
from jax.experimental.pallas import tpu as pltpu

F32 = jnp.float32
BF16 = jnp.bfloat16
HIGHEST = lax.Precision.HIGHEST

RMS_EPS = 1e-6
S5_GROUP = 16
S5_STATE = 64
S5_EIG_CLIP = 1e-4
S5_CHUNK = 16
HGRN_HEAD = 128
HGRN_CHUNK = 64
HGRN_SUB = 16
HGRN_SAFE_DECAY = 60.0
LANES = 128
SUBLANES = 8
VMEM_LIMIT = 56 * 1024 * 1024


def _cparams(*sem):
    return pltpu.CompilerParams(dimension_semantics=sem, vmem_limit_bytes=VMEM_LIMIT)


def _tile(n, want):
    t = min(n, want)
    assert n % t == 0, (n, want)
    return t


def _lower_bounds_kernel(logit_ref, out_ref):
    x = logit_ref[...]
    e = jnp.exp(x - jnp.max(x, axis=0, keepdims=True))
    p = e / jnp.sum(e, axis=0, keepdims=True)
    acc = jnp.zeros_like(p[0:1])
    rows = [acc]
    for layer in range(1, x.shape[0]):
        acc = acc + p[layer:layer + 1]
        rows.append(acc)
    out_ref[...] = jnp.concatenate(rows, axis=0)


def _lower_bounds(logits):
    return pl.pallas_call(
        _lower_bounds_kernel,
        out_shape=jax.ShapeDtypeStruct(logits.shape, F32),
        name="lower_bounds",
    )(logits.astype(F32))


def _rms(x, gain):
    return x * lax.rsqrt(jnp.mean(x * x, axis=-1, keepdims=True) + RMS_EPS) * gain


def _norm_kernel(x_ref, g_ref, o_ref):
    o_ref[...] = _rms(x_ref[...], g_ref[...]).astype(o_ref.dtype)


def _norm(x, gain, out_dtype, bm=256):
    m, d = x.shape
    bm = _tile(m, bm)
    row = pl.BlockSpec((bm, d), lambda i: (i, 0))
    vec = pl.BlockSpec((1, d), lambda i: (0, 0))
    return pl.pallas_call(
        _norm_kernel, grid=(m // bm,), in_specs=[row, vec], out_specs=row,
        out_shape=jax.ShapeDtypeStruct((m, d), out_dtype),
        compiler_params=_cparams("parallel"), name="rms_norm",
    )(x, gain.reshape(1, d))


def _resid_norm_kernel(h_ref, m_ref, gpost_ref, gpre_ref, h_out_ref, a_out_ref):
    h = h_ref[...] + _rms(m_ref[...], gpost_ref[...])
    h_out_ref[...] = h
    a_out_ref[...] = _rms(h, gpre_ref[...]).astype(a_out_ref.dtype)


def _resid_kernel(h_ref, m_ref, gpost_ref, h_out_ref):
    h_out_ref[...] = h_ref[...] + _rms(m_ref[...], gpost_ref[...])


def _resid_norm(h, m_out, g_post, g_pre, a_dtype, bm=256):
    m, d = h.shape
    bm = _tile(m, bm)
    row = pl.BlockSpec((bm, d), lambda i: (i, 0))
    vec = pl.BlockSpec((1, d), lambda i: (0, 0))
    if g_pre is None:
        return pl.pallas_call(
            _resid_kernel, grid=(m // bm,), in_specs=[row, row, vec], out_specs=row,
            out_shape=jax.ShapeDtypeStruct((m, d), F32),
            compiler_params=_cparams("parallel"), name="resid",
        )(h, m_out, g_post.reshape(1, d)), None
    return pl.pallas_call(
        _resid_norm_kernel, grid=(m // bm,), in_specs=[row, row, vec, vec],
        out_specs=[row, row],
        out_shape=[jax.ShapeDtypeStruct((m, d), F32), jax.ShapeDtypeStruct((m, d), a_dtype)],
        compiler_params=_cparams("parallel"), name="resid_norm",
    )(h, m_out, g_post.reshape(1, d), g_pre.reshape(1, d))


def _mm_kernel(*refs, n_w, n_vec, epilogue, n_split):
    x_ref = refs[0]
    w_refs = refs[1:1 + n_w]
    vec_refs = refs[1 + n_w:1 + n_w + n_vec]
    out_refs = refs[1 + n_w + n_vec:]
    x = x_ref[...]
    part = w_refs[0].shape[1] // n_split
    for p in range(n_split):
        cols = slice(p * part, (p + 1) * part)
        accs = [jnp.dot(x, w[:, cols].astype(BF16), preferred_element_type=F32) for w in w_refs]
        outs = epilogue(*accs, *[v[:, cols] for v in vec_refs])
        for o_ref, val in zip(out_refs, outs):
            o_ref[:, cols] = val.astype(o_ref.dtype)


def _matmul(x, w, layer, col_starts, n_cols, epilogue, out_dtypes, vecs=(), bm=1024, bn=256, x_buffers=2,
            n_split=1, name="matmul"):
    m, kdim = x.shape
    bm = _tile(m, bm)
    bn = _tile(n_cols, bn)
    x_spec = pl.BlockSpec((bm, kdim), lambda i, j: (i, 0), pipeline_mode=pl.Buffered(x_buffers))
    w_specs = []
    for s in col_starts:
        assert s % bn == 0
        w_specs.append(pl.BlockSpec((None, kdim, bn),
                                    functools.partial(lambda i, j, off: (layer, 0, j + off), off=s // bn)))
    vec_spec = pl.BlockSpec((1, bn), lambda i, j: (0, j))
    out_spec = pl.BlockSpec((bm, bn), lambda i, j: (i, j))
    outs = pl.pallas_call(
        functools.partial(_mm_kernel, n_w=len(col_starts), n_vec=len(vecs), epilogue=epilogue, n_split=n_split),
        grid=(m // bm, n_cols // bn),
        in_specs=[x_spec] + w_specs + [vec_spec] * len(vecs),
        out_specs=[out_spec] * len(out_dtypes),
        out_shape=[jax.ShapeDtypeStruct((m, n_cols), dt) for dt in out_dtypes],
        compiler_params=_cparams("parallel", "arbitrary"), name=name,
    )(x, *([w] * len(col_starts)), *[v.reshape(1, n_cols) for v in vecs])
    return outs


def _ep_identity(acc):
    return (acc,)


def _ep_silu(acc):
    return (acc * jax.nn.sigmoid(acc),)


def _ep_glu(val, gate):
    return (val * jax.nn.sigmoid(gate),)


def _ep_swiglu(gate, up):
    return (gate * jax.nn.sigmoid(gate) * up,)


def _ep_forget(f, lb):
    e = jnp.exp(-jnp.abs(f))
    one_e = 1.0 + e
    log_sig = jnp.minimum(f, 0.0) - jnp.log(one_e)
    a = jnp.log(lb)
    b = jnp.log1p(-lb) + log_sig
    log_forget = jnp.maximum(a, b) + jnp.log(1.0 + jnp.exp(-jnp.abs(a - b)))
    key = (1.0 - lb) * jnp.where(f >= 0.0, e, 1.0) / one_e
    return log_forget, key


def _s5_prep_kernel(are_ref, aim_ref, ldt_ref, btr_ref, bti_ref, cre_ref, cim_ref,
                    dt_ref, bac_ref, cpc_ref, al_ref):
    gb = are_ref.shape[0]
    h, big_l = S5_GROUP, S5_CHUNK
    lane = lax.broadcasted_iota(jnp.int32, (h, LANES), 1)
    for g in range(gb):
        rows = slice(g * h, (g + 1) * h)
        lam_re = jnp.minimum(are_ref[g], -S5_EIG_CLIP)
        lam_im = aim_ref[g]
        dt = jnp.exp(ldt_ref[g])
        mag = jnp.exp(lam_re * dt)
        ar = mag * jnp.cos(lam_im * dt)
        ai = mag * jnp.sin(lam_im * dt)
        den = lam_re * lam_re + lam_im * lam_im
        z_re = ((ar - 1.0) * lam_re + ai * lam_im) / den
        z_im = (ai * lam_re - (ar - 1.0) * lam_im) / den
        btr, bti = btr_ref[g], bti_ref[g]
        bbr = z_re * btr - z_im * bti
        bbi = z_re * bti + z_im * btr
        cre, cim = cre_ref[g], cim_ref[g]
        pr = jnp.ones_like(ar)
        pi = jnp.zeros_like(ar)
        cp_rows = []
        for tau in range(big_l + 1):
            cp = jnp.concatenate([cre * pr - cim * pi, -(cre * pi + cim * pr)], axis=-1)
            if tau < big_l:
                cp_rows.append(cp)
                ba = jnp.concatenate([bbr * pr - bbi * pi, bbr * pi + bbi * pr], axis=-1)
                bac_ref[tau, rows, :] = ba.astype(bac_ref.dtype)
            if tau >= 1:
                cpc_ref[tau - 1, rows, :] = cp.astype(cpc_ref.dtype)
            if tau == big_l:
                al_ref[0, g:g + 1, :] = jnp.concatenate([pr, pr], axis=-1)
                al_ref[1, g:g + 1, :] = jnp.concatenate([-pi, pi], axis=-1)
            pr, pi = pr * ar - pi * ai, pr * ai + pi * ar
        cp_all = jnp.concatenate(cp_rows, axis=0)
        bb = jnp.concatenate([bbr, bbi], axis=-1)
        krow = lax.dot_general(bb, cp_all, (((1,), (1,)), ((), ())), precision=HIGHEST,
                               preferred_element_type=F32)
        own = (lane >= g * h) & (lane < (g + 1) * h)
        taus_per_vreg = LANES // h
        for tau in range(big_l):
            src = krow[:, (tau // taus_per_vreg) * LANES:(tau // taus_per_vreg + 1) * LANES]
            shift = ((g - tau % taus_per_vreg) * h) % LANES
            moved = src if shift == 0 else pltpu.roll(src, shift, axis=1)
            dt_ref[tau, rows, :] = jnp.where(own, moved, 0.0).astype(dt_ref.dtype)


def _s5_prep(a_re, a_im, log_dt, b_re, b_im, c_re, c_im):
    g, p = a_re.shape
    h, big_l = S5_GROUP, S5_CHUNK
    gb = LANES // h
    assert 2 * p == LANES and g % gb == 0
    row = lambda z: z.astype(F32).reshape(g, 1, p)
    ldt = jnp.broadcast_to(log_dt.astype(F32)[:, None, None], (g, 1, p))
    bt = lambda z: jnp.swapaxes(z.astype(F32), 1, 2)
    vspec = pl.BlockSpec((gb, 1, p), lambda i: (i, 0, 0))
    mspec = pl.BlockSpec((gb, h, p), lambda i: (i, 0, 0))
    op_spec = pl.BlockSpec((None, big_l, LANES, LANES), lambda i: (i, 0, 0, 0))
    op_shape = jax.ShapeDtypeStruct((g // gb, big_l, LANES, LANES), BF16)
    return pl.pallas_call(
        _s5_prep_kernel, grid=(g // gb,),
        in_specs=[vspec, vspec, vspec, mspec, mspec, mspec, mspec],
        out_specs=[op_spec, op_spec, op_spec, pl.BlockSpec((None, 2, gb, LANES), lambda i: (i, 0, 0, 0))],
        out_shape=[op_shape, op_shape, op_shape, jax.ShapeDtypeStruct((g // gb, 2, gb, LANES), F32)],
        compiler_params=_cparams("parallel"), name="s5_prep",
    )(row(a_re), row(a_im), ldt, bt(b_re), bt(b_im), c_re.astype(F32), c_im.astype(F32))


def _s5_main_kernel(a_ref, dt_ref, bac_ref, cpc_ref, al_ref, y_ref,
                    u2_ref, bt_ref, bdw_ref, bdo_ref, w3_ref, w3s_ref, sp3_ref, *, batch, n_chunks):
    h, big_l = S5_GROUP, S5_CHUNK
    gb = LANES // h
    nrows = batch * n_chunks
    nseq = batch * gb

    @pl.when(pl.program_id(0) == 0)
    def _():
        bt_ref[...] = jnp.zeros_like(bt_ref)
        bdw_ref[...] = jnp.zeros_like(bdw_ref)
        bdo_ref[...] = jnp.zeros_like(bdo_ref)

    for s in range(big_l):
        for t in range(s, big_l):
            bt_ref[s * LANES:(s + 1) * LANES, t * LANES:(t + 1) * LANES] = dt_ref[t - s]
        for g in range(gb):
            rows = slice(s * LANES + g * h, s * LANES + (g + 1) * h)
            cols = slice(g * LANES, (g + 1) * LANES)
            bdw_ref[rows, cols] = bac_ref[big_l - 1 - s, g * h:(g + 1) * h, :]
            bdo_ref[rows, cols] = cpc_ref[s, g * h:(g + 1) * h, :]

    for s in range(big_l):
        u2_ref[:, s * LANES:(s + 1) * LANES] = a_ref[pl.ds(s, nrows, stride=big_l), :].astype(BF16)
    u2 = u2_ref[...]

    win = jnp.dot(u2, bdw_ref[...], preferred_element_type=F32)
    for b in range(batch):
        for g in range(gb):
            w3_ref[pl.ds(b * gb + g, n_chunks, stride=nseq), :] = (
                win[b * n_chunks:(b + 1) * n_chunks, g * LANES:(g + 1) * LANES])
    w3s_ref[...] = pltpu.roll(w3_ref[...], LANES // 2, axis=1)
    a1 = jnp.concatenate([al_ref[0]] * batch, axis=0)
    a2 = jnp.concatenate([al_ref[1]] * batch, axis=0)

    def step(c, carry):
        st, st_sw = carry
        base = pl.multiple_of(c * nseq, nseq)
        sp3_ref[pl.ds(base, nseq), :] = st
        new = a1 * st + a2 * st_sw + w3_ref[pl.ds(base, nseq), :]
        new_sw = a1 * st_sw - a2 * st + w3s_ref[pl.ds(base, nseq), :]
        return new, new_sw

    zero = jnp.zeros((nseq, LANES), F32)
    lax.fori_loop(0, n_chunks, step, (zero, zero))
    sprev = jnp.concatenate(
        [jnp.concatenate([sp3_ref[pl.ds(b * gb + g, n_chunks, stride=nseq), :] for b in range(batch)], axis=0)
         for g in range(gb)], axis=1).astype(BF16)

    nt = (((1,), (1,)), ((), ()))
    pair = 2 * LANES
    for tp in range(big_l * LANES // pair):
        cols = slice(tp * pair, (tp + 1) * pair)
        live = (tp + 1) * pair
        y2 = (jnp.dot(u2_ref[:, 0:live], bt_ref[0:live, cols], preferred_element_type=F32)
              + lax.dot_general(sprev, bdo_ref[cols, :], nt, preferred_element_type=F32))
        y_ref[pl.ds(2 * tp, nrows, stride=big_l), :] = y2[:, 0:LANES]
        y_ref[pl.ds(2 * tp + 1, nrows, stride=big_l), :] = y2[:, LANES:pair]


def _s5_core(a, batch, dt_op, bac, cpc, al):
    m, d = a.shape
    t = m // batch
    big_l = S5_CHUNK
    gb = LANES // S5_GROUP
    n_chunks = t // big_l
    nrows = batch * n_chunks
    width = big_l * LANES
    tok = pl.BlockSpec((m, LANES), lambda i: (0, i))
    op_spec = pl.BlockSpec((None, big_l, LANES, LANES), lambda i: (i, 0, 0, 0))
    return pl.pallas_call(
        functools.partial(_s5_main_kernel, batch=batch, n_chunks=n_chunks),
        grid=(d // LANES,),
        in_specs=[tok, op_spec, op_spec, op_spec, pl.BlockSpec((None, 2, gb, LANES), lambda i: (i, 0, 0, 0))],
        out_specs=tok,
        out_shape=jax.ShapeDtypeStruct((m, d), F32),
        scratch_shapes=[pltpu.VMEM((nrows, width), BF16),
                        pltpu.VMEM((width, width), BF16),
                        pltpu.VMEM((width, gb * LANES), BF16),
                        pltpu.VMEM((width, gb * LANES), BF16),
                        pltpu.VMEM((n_chunks * batch * gb, LANES), F32),
                        pltpu.VMEM((n_chunks * batch * gb, LANES), F32),
                        pltpu.VMEM((n_chunks * batch * gb, LANES), F32)],
        compiler_params=_cparams("arbitrary"), name="s5_main",
    )(a, dt_op, bac, cpc, al)


def _s5_act_kernel(y_ref, a_ref, d_ref, z_ref):
    z_ref[...] = jax.nn.gelu(y_ref[...] + d_ref[...] * a_ref[...]).astype(z_ref.dtype)


def _s5_act(y, a, d_skip, bm=256):
    m, d = y.shape
    bm = _tile(m, bm)
    row = pl.BlockSpec((bm, d), lambda i: (i, 0))
    vec = pl.BlockSpec((1, d), lambda i: (0, 0))
    return pl.pallas_call(
        _s5_act_kernel, grid=(m // bm,), in_specs=[row, row, vec], out_specs=row,
        out_shape=jax.ShapeDtypeStruct((m, d), BF16),
        compiler_params=_cparams("parallel"), name="s5_act",
    )(y, a, d_skip.astype(F32).reshape(1, d))


def _hgrn_stage1(q, k, cum, crow, vb, st, diag_mask, with_diag):
    nt = (((1,), (1,)), ((), ()))
    tn = (((0,), (0,)), ((), ()))
    r15, r31, r47, r63 = crow(15), crow(31), crow(47), crow(63)

    def zrows(n):
        return jnp.zeros((n, HGRN_HEAD), F32)

    q0 = q * jnp.exp(cum)
    q15 = q[16:32] * jnp.exp(cum[16:32] - r15)
    q31 = q[32:64] * jnp.exp(cum[32:64] - r31)
    q47 = q[48:64] * jnp.exp(cum[48:64] - r47)
    k63 = k * jnp.exp(r63 - cum)
    k31 = k[0:32] * jnp.exp(r31 - cum[0:32])
    k15 = k[0:16] * jnp.exp(r15 - cum[0:16])
    k47 = k[32:48] * jnp.exp(r47 - cum[32:48])
    o_state = lax.dot_general(q0.astype(BF16), st.astype(BF16), nt, preferred_element_type=F32)
    st_new = st * jnp.exp(r63) + lax.dot_general(vb, k63.astype(BF16), tn, preferred_element_type=F32)
    q_cat = jnp.concatenate([
        jnp.concatenate([zrows(32), q31], axis=0),
        jnp.concatenate([zrows(16), q15, zrows(32)], axis=0),
        jnp.concatenate([zrows(48), q47], axis=0)], axis=-1).astype(BF16)
    k_cat = jnp.concatenate([
        jnp.concatenate([k31, zrows(32)], axis=0),
        jnp.concatenate([k15, zrows(48)], axis=0),
        jnp.concatenate([zrows(32), k47, zrows(16)], axis=0)], axis=-1).astype(BF16)
    scores = lax.dot_general(q_cat, k_cat, nt, preferred_element_type=F32)
    if with_diag:
        q_d = jnp.concatenate([q0[0:16], q15, q31[0:16], q47], axis=0).astype(BF16)
        k_d = jnp.concatenate([
            k[0:16] * jnp.exp(jnp.minimum(-cum[0:16], HGRN_SAFE_DECAY)),
            k[16:32] * jnp.exp(jnp.minimum(r15 - cum[16:32], HGRN_SAFE_DECAY)),
            k[32:48] * jnp.exp(jnp.minimum(r31 - cum[32:48], HGRN_SAFE_DECAY)),
            k[48:64] * jnp.exp(jnp.minimum(r47 - cum[48:64], HGRN_SAFE_DECAY))], axis=0).astype(BF16)
        scores = scores + jnp.where(diag_mask, lax.dot_general(q_d, k_d, nt, preferred_element_type=F32), 0.0)
    return o_state, st_new, scores


def _hgrn_exact_diag(qtile, krow, crow):
    c_len, sub = HGRN_CHUNK, HGRN_SUB
    row8 = lax.broadcasted_iota(jnp.int32, (SUBLANES, HGRN_HEAD), 0)
    lane8 = lax.broadcasted_iota(jnp.int32, (SUBLANES, HGRN_HEAD), 1)
    pairs = [(t0, list(range((t0 // sub) * sub, t0 + SUBLANES))) for t0 in range(0, c_len, SUBLANES)]
    e_tiles = []
    for t0, s_list in pairs:
        qt = qtile(t0)
        ct = jnp.concatenate([crow(t0 + i) for i in range(SUBLANES)], axis=0)
        for s in s_list:
            e = qt * krow(s) * jnp.exp(ct - crow(s))
            if s >= t0:
                e = jnp.where(row8 >= s - t0, e, 0.0)
            e_tiles.append(e)
    e_all = jnp.concatenate(e_tiles, axis=0).astype(BF16)
    sums = jnp.dot(e_all, jnp.ones((HGRN_HEAD, HGRN_HEAD), BF16), preferred_element_type=F32)
    diag_tiles = []
    idx = 0
    for t0, s_list in pairs:
        acc = jnp.zeros((SUBLANES, HGRN_HEAD), F32)
        for s in s_list:
            acc = jnp.where(lane8 == s, sums[idx * SUBLANES:(idx + 1) * SUBLANES], acc)
            idx += 1
        diag_tiles.append(acc)
    return jnp.concatenate(diag_tiles, axis=0)[:, 0:c_len]


def _hgrn_finish(o, scores, vb, gn, sg):
    o = o + jnp.dot(scores.astype(BF16), vb, preferred_element_type=F32)
    o = o * lax.rsqrt(jnp.mean(o * o, axis=-1, keepdims=True) + RMS_EPS)
    return o * gn * sg.astype(F32)


def _hgrn_core_kernel(q_ref, k_ref, lf_ref, v_ref, sg_ref, gn_ref, o_ref, st_ref, cum_ref, kc_ref, cc_ref,
                      tri_ref, *, heads):
    c_len, sub, hd_dim = HGRN_CHUNK, HGRN_SUB, HGRN_HEAD
    tb = q_ref.shape[0]
    n_chunks = tb // c_len
    assert c_len == 4 * sub

    @pl.when(pl.program_id(2) == 0)
    def _():
        st_ref[...] = jnp.zeros_like(st_ref)
        ri = lax.broadcasted_iota(jnp.int32, (tb, tb), 0)
        ci = lax.broadcasted_iota(jnp.int32, (tb, tb), 1)
        tri = ((ri >= ci) & ((ri // c_len) == (ci // c_len))).astype(BF16)
        tri_ref[...] = jnp.concatenate([tri, tri, tri], axis=1)

    lf = lf_ref[...]
    lf_hi = lf.astype(BF16)
    rest = lf - lf_hi.astype(F32)
    lf_mid = rest.astype(BF16)
    lf_lo = (rest - lf_mid.astype(F32)).astype(BF16)
    cum_ref[...] = jnp.dot(tri_ref[...], jnp.concatenate([lf_hi, lf_mid, lf_lo], axis=0),
                           preferred_element_type=F32)
    worst = jnp.zeros((1, cum_ref.shape[1]), F32)
    for c in range(n_chunks):
        prev = worst * 0.0
        for i in range(sub - 1, c_len, sub):
            cur = cum_ref[c * c_len + i:c * c_len + i + 1, :]
            worst = jnp.maximum(worst, prev - cur)
            prev = cur
    exact_needed = jnp.max(worst) > HGRN_SAFE_DECAY

    ri = lax.broadcasted_iota(jnp.int32, (c_len, c_len), 0)
    ci = lax.broadcasted_iota(jnp.int32, (c_len, c_len), 1)
    diag_mask = (ri >= ci) & ((ri // sub) == (ci // sub))
    lanes = [slice(hd * hd_dim, (hd + 1) * hd_dim) for hd in range(heads)]

    @pl.when(jnp.logical_not(exact_needed))
    def _():
        for c in range(n_chunks):
            r0 = c * c_len
            rows = slice(r0, r0 + c_len)
            for hd, ls in enumerate(lanes):
                vb = v_ref[rows, ls].astype(BF16)
                o, st_new, scores = _hgrn_stage1(
                    q_ref[rows, ls], k_ref[rows, ls], cum_ref[rows, ls],
                    lambda i, r0=r0, ls=ls: cum_ref[r0 + i:r0 + i + 1, ls],
                    vb, st_ref[hd], diag_mask, True)
                st_ref[hd] = st_new
                o_ref[rows, ls] = _hgrn_finish(o, scores, vb, gn_ref[:, ls], sg_ref[rows, ls]).astype(o_ref.dtype)

    @pl.when(exact_needed)
    def _():
        def chunk_body(c, carry):
            r0 = pl.multiple_of(c * c_len, c_len)
            rows = pl.ds(r0, c_len)
            cc_ref[...] = cum_ref[rows, :]
            kc_ref[...] = k_ref[rows, :]
            for hd, ls in enumerate(lanes):
                vb = v_ref[rows, ls].astype(BF16)
                o, st_new, scores = _hgrn_stage1(
                    q_ref[rows, ls], kc_ref[:, ls], cc_ref[:, ls], lambda i, ls=ls: cc_ref[i:i + 1, ls],
                    vb, st_ref[hd], diag_mask, False)
                st_ref[hd] = st_new
                scores = scores + _hgrn_exact_diag(
                    lambda t0, ls=ls: q_ref[pl.ds(r0 + t0, SUBLANES), ls],
                    lambda s, ls=ls: kc_ref[s:s + 1, ls],
                    lambda s, ls=ls: cc_ref[s:s + 1, ls])
                o_ref[rows, ls] = _hgrn_finish(o, scores, vb, gn_ref[:, ls], sg_ref[rows, ls]).astype(o_ref.dtype)
            return carry

        lax.fori_loop(0, n_chunks, chunk_body, 0)


def _hgrn_core(q, k, log_f, v, sg, g_norm, batch, tb=256, heads=4):
    m, d = q.shape
    t = m // batch
    tb = _tile(t, tb)
    heads = _tile(d // HGRN_HEAD, heads)
    wl = heads * HGRN_HEAD
    nt = t // tb
    blk = pl.BlockSpec((tb, wl), lambda b, j, i: (b * nt + i, j))
    vec = pl.BlockSpec((1, wl), lambda b, j, i: (0, j))
    return pl.pallas_call(
        functools.partial(_hgrn_core_kernel, heads=heads),
        grid=(batch, d // wl, nt),
        in_specs=[blk, blk, blk, blk, blk, vec],
        out_specs=blk,
        out_shape=jax.ShapeDtypeStruct((m, d), BF16),
        scratch_shapes=[pltpu.VMEM((heads, HGRN_HEAD, HGRN_HEAD), F32),
                        pltpu.VMEM((tb, wl), F32),
                        pltpu.VMEM((HGRN_CHUNK, wl), F32),
                        pltpu.VMEM((HGRN_CHUNK, wl), F32),
                        pltpu.VMEM((tb, 3 * tb), BF16)],
        compiler_params=_cparams("parallel", "parallel", "arbitrary"), name="hgrn_core",
    )(q, k, log_f, v, sg, g_norm.astype(F32).reshape(1, d))


def kernel(x, norm_gains, s5_a_re, s5_a_im, s5_log_dt, s5_b_re, s5_b_im, s5_c_re, s5_c_im, s5_d,
           s5_w_glu, hgrn_w_in, hgrn_lb_logits, hgrn_g_norm, hgrn_w_out, ffn_w_gate_up, ffn_w_down):
    batch, seq, d = x.shape
    depth = norm_gains.shape[0]
    d_ff = ffn_w_down.shape[1]
    m = batch * seq
    gains = norm_gains.astype(F32)
    lower_bounds = _lower_bounds(hgrn_lb_logits)
    w_down = ffn_w_down.astype(BF16)

    h = x.reshape(m, d).astype(F32)
    a = _norm(h, gains[0, 0], F32)
    for layer in range(depth):
        j = layer // 2
        if layer % 2 == 0:
            s5_ops = _s5_prep(s5_a_re[j], s5_a_im[j], s5_log_dt[j], s5_b_re[j], s5_b_im[j],
                              s5_c_re[j], s5_c_im[j])
            y = _s5_core(a, batch, *s5_ops)
            z = _s5_act(y, a, s5_d[j])
            (mix,) = _matmul(z, s5_w_glu, j, (0, d), d, _ep_glu, (F32,), name="s5_glu")
        else:
            (q,) = _matmul(a, hgrn_w_in, j, (0,), d, _ep_silu, (F32,), bn=512, name="hgrn_q")
            log_f, key = _matmul(a, hgrn_w_in, j, (d,), d, _ep_forget, (F32, F32), vecs=(lower_bounds[layer],),
                                 bn=512, n_split=2, name="hgrn_f")
            (val,) = _matmul(a, hgrn_w_in, j, (2 * d,), d, _ep_identity, (F32,), bn=512, name="hgrn_v")
            (sg,) = _matmul(a, hgrn_w_in, j, (3 * d,), d, _ep_silu, (BF16,), bn=512, name="hgrn_g")
            o = _hgrn_core(q, key, log_f, val, sg, hgrn_g_norm[j], batch)
            (mix,) = _matmul(o, hgrn_w_out, j, (0,), d, _ep_identity, (F32,), bn=512, name="hgrn_out")
        h, a_ffn = _resid_norm(h, mix, gains[layer, 1], gains[layer, 2], BF16)
        (act,) = _matmul(a_ffn, ffn_w_gate_up, layer, (0, d_ff), d_ff, _ep_swiglu, (BF16,), bm=2048, name="ffn_up")
        (f_out,) = _matmul(act, w_down, layer, (0,), d, _ep_identity, (F32,), x_buffers=1, name="ffn_down")
        if layer + 1 < depth:
            a_dtype = F32 if (layer + 1) % 2 == 0 else BF16
            h, a = _resid_norm(h, f_out, gains[layer, 3], gains[layer + 1, 0], a_dtype)
        else:
            h, _ = _resid_norm(h, f_out, gains[layer, 3], None, None)
    return h.reshape(batch, seq, d).astype(x.dtype)
```

```python
import functools

import jax
import jax.numpy as jnp
from jax import lax
from jax.experimental import pallas as pl
from jax.experimental.pallas import tpu as pltpu

F32 = jnp.float32
BF16 = jnp.bfloat16
HIGHEST = lax.Precision.HIGHEST

RMS_EPS = 1e-6
S5_GROUP = 16
S5_STATE = 64
S5_EIG_CLIP = 1e-4
S5_CHUNK = 16
HGRN_HEAD = 128
HGRN_CHUNK = 64
HGRN_SUB = 16
HGRN_SAFE_DECAY = 60.0
LANES = 128
SUBLANES = 8
VMEM_LIMIT = 56 * 1024 * 1024


def _cparams(*sem):
    return pltpu.CompilerParams(dimension_semantics=sem, vmem_limit_bytes=VMEM_LIMIT)


def _tile(n, want):
    t = min(n, want)
    assert n % t == 0, (n, want)
    return t


def _lower_bounds_kernel(logit_ref, out_ref):
    x = logit_ref[...]
    e = jnp.exp(x - jnp.max(x, axis=0, keepdims=True))
    p = e / jnp.sum(e, axis=0, keepdims=True)
    acc = jnp.zeros_like(p[0:1])
    rows = [acc]
    for layer in range(1, x.shape[0]):
        acc = acc + p[layer:layer + 1]
        rows.append(acc)
    out_ref[...] = jnp.concatenate(rows, axis=0)


def _lower_bounds(logits):
    return pl.pallas_call(
        _lower_bounds_kernel,
        out_shape=jax.ShapeDtypeStruct(logits.shape, F32),
        name="lower_bounds",
    )(logits.astype(F32))


def _rms(x, gain):
    return x * lax.rsqrt(jnp.mean(x * x, axis=-1, keepdims=True) + RMS_EPS) * gain


def _norm_kernel(x_ref, g_ref, o_ref):
    o_ref[...] = _rms(x_ref[...], g_ref[...]).astype(o_ref.dtype)


def _norm(x, gain, out_dtype, bm=256):
    m, d = x.shape
    bm = _tile(m, bm)
    row = pl.BlockSpec((bm, d), lambda i: (i, 0))
    vec = pl.BlockSpec((1, d), lambda i: (0, 0))
    return pl.pallas_call(
        _norm_kernel, grid=(m // bm,), in_specs=[row, vec], out_specs=row,
        out_shape=jax.ShapeDtypeStruct((m, d), out_dtype),
        compiler_params=_cparams("parallel"), name="rms_norm",
    )(x, gain.reshape(1, d))


def _resid_norm_kernel(h_ref, m_ref, gpost_ref, gpre_ref, h_out_ref, a_out_ref):
    h = h_ref[...] + _rms(m_ref[...].astype(F32), gpost_ref[...])
    h_out_ref[...] = h
    a_out_ref[...] = _rms(h, gpre_ref[...]).astype(a_out_ref.dtype)


def _resid_kernel(h_ref, m_ref, gpost_ref, h_out_ref):
    h_out_ref[...] = h_ref[...] + _rms(m_ref[...].astype(F32), gpost_ref[...])


def _resid_norm(h, m_out, g_post, g_pre, a_dtype, bm=256):
    m, d = h.shape
    bm = _tile(m, bm)
    row = pl.BlockSpec((bm, d), lambda i: (i, 0))
    vec = pl.BlockSpec((1, d), lambda i: (0, 0))
    if g_pre is None:
        return pl.pallas_call(
            _resid_kernel, grid=(m // bm,), in_specs=[row, row, vec], out_specs=row,
            out_shape=jax.ShapeDtypeStruct((m, d), F32),
            compiler_params=_cparams("parallel"), name="resid",
        )(h, m_out, g_post.reshape(1, d)), None
    return pl.pallas_call(
        _resid_norm_kernel, grid=(m // bm,), in_specs=[row, row, vec, vec],
        out_specs=[row, row],
        out_shape=[jax.ShapeDtypeStruct((m, d), F32), jax.ShapeDtypeStruct((m, d), a_dtype)],
        compiler_params=_cparams("parallel"), name="resid_norm",
    )(h, m_out, g_post.reshape(1, d), g_pre.reshape(1, d))


def _mm_kernel(*refs, n_w, n_vec, epilogue, m_split):
    x_ref = refs[0]
    w_refs = refs[1:1 + n_w]
    vec_refs = refs[1 + n_w:1 + n_w + n_vec]
    out_refs = refs[1 + n_w + n_vec:]
    ws = [w[...].astype(BF16) for w in w_refs]
    vecs = [v[...] for v in vec_refs]
    part = x_ref.shape[0] // m_split
    for p in range(m_split):
        rows = slice(p * part, (p + 1) * part)
        accs = [jnp.dot(x_ref[rows, :], w, preferred_element_type=F32) for w in ws]
        outs = epilogue(*accs, *vecs)
        for o_ref, val in zip(out_refs, outs):
            o_ref[rows, :] = val.astype(o_ref.dtype)


def _matmul(x, w, layer, col_starts, n_cols, epilogue, out_dtypes, vecs=(), bm=1024, bn=256, x_buffers=2,
            m_split=1, name="matmul"):
    m, kdim = x.shape
    bm = _tile(m, bm)
    bn = _tile(n_cols, bn)
    x_spec = pl.BlockSpec((bm, kdim), lambda i, j: (i, 0), pipeline_mode=pl.Buffered(x_buffers))
    w_specs = []
    for s in col_starts:
        assert s % bn == 0
        w_specs.append(pl.BlockSpec((None, kdim, bn),
                                    functools.partial(lambda i, j, off: (layer, 0, j + off), off=s // bn)))
    vec_spec = pl.BlockSpec((1, bn), lambda i, j: (0, j))
    out_spec = pl.BlockSpec((bm, bn), lambda i, j: (i, j))
    outs = pl.pallas_call(
        functools.partial(_mm_kernel, n_w=len(col_starts), n_vec=len(vecs), epilogue=epilogue, m_split=m_split),
        grid=(m // bm, n_cols // bn),
        in_specs=[x_spec] + w_specs + [vec_spec] * len(vecs),
        out_specs=[out_spec] * len(out_dtypes),
        out_shape=[jax.ShapeDtypeStruct((m, n_cols), dt) for dt in out_dtypes],
        compiler_params=_cparams("parallel", "arbitrary"), name=name,
    )(x, *([w] * len(col_starts)), *[v.reshape(1, n_cols) for v in vecs])
    return outs


def _ep_identity(acc):
    return (acc,)


def _ep_silu(acc):
    return (acc * jax.nn.sigmoid(acc),)


def _ep_glu(val, gate):
    return (val * jax.nn.sigmoid(gate),)


def _ep_swiglu(gate, up):
    return (gate * jax.nn.sigmoid(gate) * up,)


def _ep_forget(f, lb):
    e = jnp.exp(-jnp.abs(f))
    one_e = 1.0 + e
    log_sig = jnp.minimum(f, 0.0) - jnp.log(one_e)
    a = jnp.log(lb)
    b = jnp.log1p(-lb) + log_sig
    log_forget = jnp.maximum(a, b) + jnp.log(1.0 + jnp.exp(-jnp.abs(a - b)))
    key = (1.0 - lb) * jnp.where(f >= 0.0, e, 1.0) / one_e
    return log_forget, key


def _s5_prep_kernel(are_ref, aim_ref, ldt_ref, btr_ref, bti_ref, cre_ref, cim_ref,
                    dt_ref, bac_ref, cpc_ref, al_ref):
    gb = are_ref.shape[0]
    h, big_l = S5_GROUP, S5_CHUNK
    lane = lax.broadcasted_iota(jnp.int32, (h, LANES), 1)
    for g in range(gb):
        rows = slice(g * h, (g + 1) * h)
        lam_re = jnp.minimum(are_ref[g], -S5_EIG_CLIP)
        lam_im = aim_ref[g]
        dt = jnp.exp(ldt_ref[g])
        mag = jnp.exp(lam_re * dt)
        ar = mag * jnp.cos(lam_im * dt)
        ai = mag * jnp.sin(lam_im * dt)
        den = lam_re * lam_re + lam_im * lam_im
        z_re = ((ar - 1.0) * lam_re + ai * lam_im) / den
        z_im = (ai * lam_re - (ar - 1.0) * lam_im) / den
        btr, bti = btr_ref[g], bti_ref[g]
        bbr = z_re * btr - z_im * bti
        bbi = z_re * bti + z_im * btr
        cre, cim = cre_ref[g], cim_ref[g]
        pr = jnp.ones_like(ar)
        pi = jnp.zeros_like(ar)
        cp_rows = []
        for tau in range(big_l + 1):
            cp = jnp.concatenate([cre * pr - cim * pi, -(cre * pi + cim * pr)], axis=-1)
            if tau < big_l:
                cp_rows.append(cp)
                ba = jnp.concatenate([bbr * pr - bbi * pi, bbr * pi + bbi * pr], axis=-1)
                bac_ref[tau, rows, :] = ba.astype(bac_ref.dtype)
            if tau >= 1:
                cpc_ref[tau - 1, rows, :] = cp.astype(cpc_ref.dtype)
            if tau == big_l:
                al_ref[0, g:g + 1, :] = jnp.concatenate([pr, pr], axis=-1)
                al_ref[1, g:g + 1, :] = jnp.concatenate([-pi, pi], axis=-1)
            pr, pi = pr * ar - pi * ai, pr * ai + pi * ar
        cp_all = jnp.concatenate(cp_rows, axis=0)
        bb = jnp.concatenate([bbr, bbi], axis=-1)
        krow = lax.dot_general(bb, cp_all, (((1,), (1,)), ((), ())), precision=HIGHEST,
                               preferred_element_type=F32)
        own = (lane >= g * h) & (lane < (g + 1) * h)
        taus_per_vreg = LANES // h
        for tau in range(big_l):
            src = krow[:, (tau // taus_per_vreg) * LANES:(tau // taus_per_vreg + 1) * LANES]
            shift = ((g - tau % taus_per_vreg) * h) % LANES
            moved = src if shift == 0 else pltpu.roll(src, shift, axis=1)
            dt_ref[tau, rows, :] = jnp.where(own, moved, 0.0).astype(dt_ref.dtype)


def _s5_prep(a_re, a_im, log_dt, b_re, b_im, c_re, c_im):
    g, p = a_re.shape
    h, big_l = S5_GROUP, S5_CHUNK
    gb = LANES // h
    assert 2 * p == LANES and g % gb == 0
    row = lambda z: z.astype(F32).reshape(g, 1, p)
    ldt = jnp.broadcast_to(log_dt.astype(F32)[:, None, None], (g, 1, p))
    bt = lambda z: jnp.swapaxes(z.astype(F32), 1, 2)
    vspec = pl.BlockSpec((gb, 1, p), lambda i: (i, 0, 0))
    mspec = pl.BlockSpec((gb, h, p), lambda i: (i, 0, 0))
    op_spec = pl.BlockSpec((None, big_l, LANES, LANES), lambda i: (i, 0, 0, 0))
    op_shape = jax.ShapeDtypeStruct((g // gb, big_l, LANES, LANES), BF16)
    return pl.pallas_call(
        _s5_prep_kernel, grid=(g // gb,),
        in_specs=[vspec, vspec, vspec, mspec, mspec, mspec, mspec],
        out_specs=[op_spec, op_spec, op_spec, pl.BlockSpec((None, 2, gb, LANES), lambda i: (i, 0, 0, 0))],
        out_shape=[op_shape, op_shape, op_shape, jax.ShapeDtypeStruct((g // gb, 2, gb, LANES), F32)],
        compiler_params=_cparams("parallel"), name="s5_prep",
    )(row(a_re), row(a_im), ldt, bt(b_re), bt(b_im), c_re.astype(F32), c_im.astype(F32))


def _s5_main_kernel(a_ref, dt_ref, bac_ref, cpc_ref, al_ref, y_ref,
                    u2_ref, bt_ref, bdw_ref, bdo_ref, w3_ref, w3s_ref, sp3_ref, *, batch, n_chunks):
    h, big_l = S5_GROUP, S5_CHUNK
    gb = LANES // h
    nrows = batch * n_chunks
    nseq = batch * gb

    @pl.when(pl.program_id(0) == 0)
    def _():
        bt_ref[...] = jnp.zeros_like(bt_ref)
        bdw_ref[...] = jnp.zeros_like(bdw_ref)
        bdo_ref[...] = jnp.zeros_like(bdo_ref)

    for s in range(big_l):
        for t in range(s, big_l):
            bt_ref[s * LANES:(s + 1) * LANES, t * LANES:(t + 1) * LANES] = dt_ref[t - s]
        for g in range(gb):
            rows = slice(s * LANES + g * h, s * LANES + (g + 1) * h)
            cols = slice(g * LANES, (g + 1) * LANES)
            bdw_ref[rows, cols] = bac_ref[big_l - 1 - s, g * h:(g + 1) * h, :]
            bdo_ref[rows, cols] = cpc_ref[s, g * h:(g + 1) * h, :]

    for s in range(big_l):
        u2_ref[:, s * LANES:(s + 1) * LANES] = a_ref[pl.ds(s, nrows, stride=big_l), :].astype(BF16)
    u2 = u2_ref[...]

    win = jnp.dot(u2, bdw_ref[...], preferred_element_type=F32)
    for b in range(batch):
        for g in range(gb):
            w3_ref[pl.ds(b * gb + g, n_chunks, stride=nseq), :] = (
                win[b * n_chunks:(b + 1) * n_chunks, g * LANES:(g + 1) * LANES])
    w3s_ref[...] = pltpu.roll(w3_ref[...], LANES // 2, axis=1)
    a1 = jnp.concatenate([al_ref[0]] * batch, axis=0)
    a2 = jnp.concatenate([al_ref[1]] * batch, axis=0)

    def step(c, carry):
        st, st_sw = carry
        base = pl.multiple_of(c * nseq, nseq)
        sp3_ref[pl.ds(base, nseq), :] = st
        new = a1 * st + a2 * st_sw + w3_ref[pl.ds(base, nseq), :]
        new_sw = a1 * st_sw - a2 * st + w3s_ref[pl.ds(base, nseq), :]
        return new, new_sw

    zero = jnp.zeros((nseq, LANES), F32)
    lax.fori_loop(0, n_chunks, step, (zero, zero), unroll=True)
    sprev =jnp.concatenate(
        [jnp.concatenate([sp3_ref[pl.ds(b * gb + g, n_chunks, stride=nseq), :] for b in range(batch)], axis=0)
         for g in range(gb)], axis=1).astype(BF16)

    nt = (((1,), (1,)), ((), ()))
    pair = 2 * LANES
    for tp in range(big_l * LANES // pair):
        cols = slice(tp * pair, (tp + 1) * pair)
        live = (tp + 1) * pair
        y2 = (jnp.dot(u2_ref[:, 0:live], bt_ref[0:live, cols], preferred_element_type=F32)
              + lax.dot_general(sprev, bdo_ref[cols, :], nt, preferred_element_type=F32))
        y_ref[pl.ds(2 * tp, nrows, stride=big_l), :] = y2[:, 0:LANES]
        y_ref[pl.ds(2 * tp + 1, nrows, stride=big_l), :] = y2[:, LANES:pair]


def _s5_core(a, batch, dt_op, bac, cpc, al):
    m, d = a.shape
    t = m // batch
    big_l = S5_CHUNK
    gb = LANES // S5_GROUP
    n_chunks = t // big_l
    nrows = batch * n_chunks
    width = big_l * LANES
    tok = pl.BlockSpec((m, LANES), lambda i: (0, i))
    op_spec = pl.BlockSpec((None, big_l, LANES, LANES), lambda i: (i, 0, 0, 0))
    return pl.pallas_call(
        functools.partial(_s5_main_kernel, batch=batch, n_chunks=n_chunks),
        grid=(d // LANES,),
        in_specs=[tok, op_spec, op_spec, op_spec, pl.BlockSpec((None, 2, gb, LANES), lambda i: (i, 0, 0, 0))],
        out_specs=tok,
        out_shape=jax.ShapeDtypeStruct((m, d), F32),
        scratch_shapes=[pltpu.VMEM((nrows, width), BF16),
                        pltpu.VMEM((width, width), BF16),
                        pltpu.VMEM((width, gb * LANES), BF16),
                        pltpu.VMEM((width, gb * LANES), BF16),
                        pltpu.VMEM((n_chunks * batch * gb, LANES), F32),
                        pltpu.VMEM((n_chunks * batch * gb, LANES), F32),
                        pltpu.VMEM((n_chunks * batch * gb, LANES), F32)],
        compiler_params=_cparams("arbitrary"), name="s5_main",
    )(a, dt_op, bac, cpc, al)


def _s5_act_kernel(y_ref, a_ref, d_ref, z_ref):
    z_ref[...] = jax.nn.gelu(y_ref[...] + d_ref[...] * a_ref[...]).astype(z_ref.dtype)


def _s5_act(y, a, d_skip, bm=256):
    m, d = y.shape
    bm = _tile(m, bm)
    row = pl.BlockSpec((bm, d), lambda i: (i, 0))
    vec = pl.BlockSpec((1, d), lambda i: (0, 0))
    return pl.pallas_call(
        _s5_act_kernel, grid=(m // bm,), in_specs=[row, row, vec], out_specs=row,
        out_shape=jax.ShapeDtypeStruct((m, d), BF16),
        compiler_params=_cparams("parallel"), name="s5_act",
    )(y, a, d_skip.astype(F32).reshape(1, d))


def _hgrn_stage1(q, k, cum, crow, vb, st, diag_mask, with_diag):
    nt = (((1,), (1,)), ((), ()))
    tn = (((0,), (0,)), ((), ()))
    r15, r31, r47, r63 = crow(15), crow(31), crow(47), crow(63)

    def zrows(n):
        return jnp.zeros((n, HGRN_HEAD), F32)

    q0 = q * jnp.exp(cum)
    q15 = q[16:32] * jnp.exp(cum[16:32] - r15)
    q31 = q[32:64] * jnp.exp(cum[32:64] - r31)
    q47 = q[48:64] * jnp.exp(cum[48:64] - r47)
    k63 = k * jnp.exp(r63 - cum)
    k31 = k[0:32] * jnp.exp(r31 - cum[0:32])
    k15 = k[0:16] * jnp.exp(r15 - cum[0:16])
    k47 = k[32:48] * jnp.exp(r47 - cum[32:48])
    o_state = lax.dot_general(q0.astype(BF16), st.astype(BF16), nt, preferred_element_type=F32)
    st_new = st * jnp.exp(r63) + lax.dot_general(vb, k63.astype(BF16), tn, preferred_element_type=F32)
    q_cat = jnp.concatenate([
        jnp.concatenate([zrows(32), q31], axis=0),
        jnp.concatenate([zrows(16), q15, zrows(32)], axis=0),
        jnp.concatenate([zrows(48), q47], axis=0)], axis=-1).astype(BF16)
    k_cat = jnp.concatenate([
        jnp.concatenate([k31, zrows(32)], axis=0),
        jnp.concatenate([k15, zrows(48)], axis=0),
        jnp.concatenate([zrows(32), k47, zrows(16)], axis=0)], axis=-1).astype(BF16)
    scores = lax.dot_general(q_cat, k_cat, nt, preferred_element_type=F32)
    if with_diag:
        q_d = jnp.concatenate([q0[0:16], q15, q31[0:16], q47], axis=0).astype(BF16)
        k_d = jnp.concatenate([
            k[0:16] * jnp.exp(jnp.minimum(-cum[0:16], HGRN_SAFE_DECAY)),
            k[16:32] * jnp.exp(jnp.minimum(r15 - cum[16:32], HGRN_SAFE_DECAY)),
            k[32:48] * jnp.exp(jnp.minimum(r31 - cum[32:48], HGRN_SAFE_DECAY)),
            k[48:64] * jnp.exp(jnp.minimum(r47 - cum[48:64], HGRN_SAFE_DECAY))], axis=0).astype(BF16)
        scores = scores + jnp.where(diag_mask, lax.dot_general(q_d, k_d, nt, preferred_element_type=F32), 0.0)
    return o_state, st_new, scores


def _hgrn_exact_diag(qtile, krow, crow):
    c_len, sub = HGRN_CHUNK, HGRN_SUB
    row8 = lax.broadcasted_iota(jnp.int32, (SUBLANES, HGRN_HEAD), 0)
    lane8 = lax.broadcasted_iota(jnp.int32, (SUBLANES, HGRN_HEAD), 1)
    pairs = [(t0, list(range((t0 // sub) * sub, t0 + SUBLANES))) for t0 in range(0, c_len, SUBLANES)]
    e_tiles = []
    for t0, s_list in pairs:
        qt = qtile(t0)
        ct = jnp.concatenate([crow(t0 + i) for i in range(SUBLANES)], axis=0)
        for s in s_list:
            e = qt * krow(s) * jnp.exp(ct - crow(s))
            if s >= t0:
                e = jnp.where(row8 >= s - t0, e, 0.0)
            e_tiles.append(e)
    e_all = jnp.concatenate(e_tiles, axis=0).astype(BF16)
    sums = jnp.dot(e_all, jnp.ones((HGRN_HEAD, HGRN_HEAD), BF16), preferred_element_type=F32)
    diag_tiles = []
    idx = 0
    for t0, s_list in pairs:
        acc = jnp.zeros((SUBLANES, HGRN_HEAD), F32)
        for s in s_list:
            acc = jnp.where(lane8 == s, sums[idx * SUBLANES:(idx + 1) * SUBLANES], acc)
            idx += 1
        diag_tiles.append(acc)
    return jnp.concatenate(diag_tiles, axis=0)[:, 0:c_len]


def _hgrn_finish(o, scores, vb, gn, sg):
    o = o + jnp.dot(scores.astype(BF16), vb, preferred_element_type=F32)
    o = o * lax.rsqrt(jnp.mean(o * o, axis=-1, keepdims=True) + RMS_EPS)
    return o * gn * sg.astype(F32)


def _hgrn_core_kernel(q_ref, k_ref, lf_ref, v_ref, sg_ref, gn_ref, o_ref, st_ref, cum_ref, kc_ref, cc_ref,
                      tri_ref, *, heads):
    c_len, sub, hd_dim = HGRN_CHUNK, HGRN_SUB, HGRN_HEAD
    tb = q_ref.shape[0]
    n_chunks = tb // c_len
    assert c_len == 4 * sub

    ri = lax.broadcasted_iota(jnp.int32, (c_len, c_len), 0)
    ci = lax.broadcasted_iota(jnp.int32, (c_len, c_len), 1)
    diag_mask = (ri >= ci) & ((ri // sub) == (ci // sub))
    lanes = [slice(hd * hd_dim, (hd + 1) * hd_dim) for hd in range(heads)]

    @pl.when(pl.program_id(2) == 0)
    def _():
        st_ref[...] = jnp.zeros_like(st_ref)
        rb = lax.broadcasted_iota(jnp.int32, (tb, tb), 0)
        cb = lax.broadcasted_iota(jnp.int32, (tb, tb), 1)
        tri = ((rb >= cb) & ((rb // c_len) == (cb // c_len))).astype(BF16)
        tri_ref[...] = jnp.concatenate([tri, tri, tri], axis=1)

    lf = lf_ref[...]
    lf_hi = lf.astype(BF16)
    rest = lf - lf_hi.astype(F32)
    lf_mid = rest.astype(BF16)
    lf_lo = (rest - lf_mid.astype(F32)).astype(BF16)
    cum_ref[...] = jnp.dot(tri_ref[...], jnp.concatenate([lf_hi, lf_mid, lf_lo], axis=0),
                           preferred_element_type=F32)
    block_decay = -jnp.sum(lf.reshape(tb // sub, sub, lf.shape[1]), axis=1)
    exact_needed = jnp.max(block_decay) > HGRN_SAFE_DECAY

    @pl.when(jnp.logical_not(exact_needed))
    def _():
        for c in range(n_chunks):
            r0 = c * c_len
            rows = slice(r0, r0 + c_len)
            for hd, ls in enumerate(lanes):
                vb = v_ref[rows, ls].astype(BF16)
                o, st_new, scores = _hgrn_stage1(
                    q_ref[rows, ls], k_ref[rows, ls], cum_ref[rows, ls],
                    lambda i, r0=r0, ls=ls: cum_ref[r0 + i:r0 + i + 1, ls],
                    vb, st_ref[hd], diag_mask, True)
                st_ref[hd] = st_new
                o_ref[rows, ls] = _hgrn_finish(o, scores, vb, gn_ref[:, ls], sg_ref[rows, ls]).astype(o_ref.dtype)

    @pl.when(exact_needed)
    def _():
        def chunk_body(c, carry):
            r0 = pl.multiple_of(c * c_len, c_len)
            rows = pl.ds(r0, c_len)
            cc_ref[...] = cum_ref[rows, :]
            kc_ref[...] = k_ref[rows, :]
            for hd, ls in enumerate(lanes):
                vb = v_ref[rows, ls].astype(BF16)
                o, st_new, scores = _hgrn_stage1(
                    q_ref[rows, ls], kc_ref[:, ls], cc_ref[:, ls], lambda i, ls=ls: cc_ref[i:i + 1, ls],
                    vb, st_ref[hd], diag_mask, False)
                st_ref[hd] = st_new
                scores = scores + _hgrn_exact_diag(
                    lambda t0, ls=ls: q_ref[pl.ds(r0 + t0, SUBLANES), ls],
                    lambda s, ls=ls: kc_ref[s:s + 1, ls],
                    lambda s, ls=ls: cc_ref[s:s + 1, ls])
                o_ref[rows, ls] = _hgrn_finish(o, scores, vb, gn_ref[:, ls], sg_ref[rows, ls]).astype(o_ref.dtype)
            return carry

        lax.fori_loop(0, n_chunks, chunk_body, 0)


def _hgrn_core(q, k, log_f, v, sg, g_norm, batch, tb=256, heads=4):
    m, d = q.shape
    t = m // batch
    tb = _tile(t, tb)
    heads = _tile(d // HGRN_HEAD, heads)
    wl = heads * HGRN_HEAD
    nt = t // tb
    blk = pl.BlockSpec((tb, wl), lambda b, j, i: (b * nt + i, j))
    vec = pl.BlockSpec((1, wl), lambda b, j, i: (0, j))
    return pl.pallas_call(
        functools.partial(_hgrn_core_kernel, heads=heads),
        grid=(batch, d // wl, nt),
        in_specs=[blk, blk, blk, blk, blk, vec],
        out_specs=blk,
        out_shape=jax.ShapeDtypeStruct((m, d), BF16),
        scratch_shapes=[pltpu.VMEM((heads, HGRN_HEAD, HGRN_HEAD), F32),
                        pltpu.VMEM((tb, wl), F32),
                        pltpu.VMEM((HGRN_CHUNK, wl), F32),
                        pltpu.VMEM((HGRN_CHUNK, wl), F32),
                        pltpu.VMEM((tb, 3 * tb), BF16)],
        compiler_params=_cparams("parallel", "parallel", "arbitrary"), name="hgrn_core",
    )(q, k, log_f, v, sg, g_norm.astype(F32).reshape(1, d))


def kernel(x, norm_gains, s5_a_re, s5_a_im, s5_log_dt, s5_b_re, s5_b_im, s5_c_re, s5_c_im, s5_d,
           s5_w_glu, hgrn_w_in, hgrn_lb_logits, hgrn_g_norm, hgrn_w_out, ffn_w_gate_up, ffn_w_down):
    batch, seq, d = x.shape
    depth = norm_gains.shape[0]
    d_ff = ffn_w_down.shape[1]
    m = batch * seq
    gains = norm_gains.astype(F32)
    lower_bounds = _lower_bounds(hgrn_lb_logits)

    h = x.reshape(m, d).astype(F32)
    a = _norm(h, gains[0, 0], F32)
    for layer in range(depth):
        j = layer // 2
        if layer % 2 == 0:
            s5_ops = _s5_prep(s5_a_re[j], s5_a_im[j], s5_log_dt[j], s5_b_re[j], s5_b_im[j],
                              s5_c_re[j], s5_c_im[j])
            y = _s5_core(a, batch, *s5_ops)
            z = _s5_act(y, a, s5_d[j])
            (mix,) = _matmul(z, s5_w_glu, j, (0, d), d, _ep_glu, (BF16,), bm=2048, name="s5_glu")
        else:
            (q,) = _matmul(a, hgrn_w_in, j, (0,), d, _ep_silu, (F32,), bn=512, name="hgrn_q")
            log_f, key = _matmul(a, hgrn_w_in, j, (d,), d, _ep_forget, (F32, F32), vecs=(lower_bounds[layer],),
                                 bm=2048, m_split=2, name="hgrn_f")
            (val,) = _matmul(a, hgrn_w_in, j, (2 * d,), d, _ep_identity, (F32,), bn=512, name="hgrn_v")
            (sg,) = _matmul(a, hgrn_w_in, j, (3 * d,), d, _ep_silu, (BF16,), bn=512, name="hgrn_g")
            o = _hgrn_core(q, key, log_f, val, sg, hgrn_g_norm[j], batch)
            (mix,) = _matmul(o, hgrn_w_out, j, (0,), d, _ep_identity, (BF16,), bn=512, name="hgrn_out")
        h, a_ffn = _resid_norm(h, mix, gains[layer, 1], gains[layer, 2], BF16)
        (act,) = _matmul(a_ffn, ffn_w_gate_up, layer, (0, d_ff), d_ff, _ep_swiglu, (BF16,), bm=2048, name="ffn_up")
        (f_out,) = _matmul(act, ffn_w_down, layer, (0,), d, _ep_identity, (BF16,), x_buffers=1, name="ffn_down")
        if layer + 1 < depth:
            a_dtype = F32 if (layer + 1) % 2 == 0 else BF16
            h, a = _resid_norm(h, f_out, gains[layer, 3], gains[layer + 1, 0], a_dtype)
        else:
            h, _ = _resid_norm(h, f_out, gains[layer, 3], None, None)
    return h.reshape(batch, seq, d).astype(x.dtype)
```

```python
import functools

import jax
import jax.numpy as jnp
from jax import lax
from jax.experimental import pallas as pl
from jax.experimental.pallas import tpu as pltpu

F32 = jnp.float32
BF16 = jnp.bfloat16
HIGHEST = lax.Precision.HIGHEST

RMS_EPS = 1e-6
S5_GROUP = 16
S5_STATE = 64
S5_EIG_CLIP = 1e-4
S5_CHUNK = 16
HGRN_HEAD = 128
HGRN_CHUNK = 64
HGRN_SUB = 16
HGRN_SAFE_DECAY = 60.0
LANES = 128
SUBLANES = 8
VMEM_LIMIT = 56 * 1024 * 1024


def _cparams(*sem):
    return pltpu.CompilerParams(dimension_semantics=sem, vmem_limit_bytes=VMEM_LIMIT)


def _tile(n, want):
    t = min(n, want)
    assert n % t == 0, (n, want)
    return t


def _lower_bounds_kernel(logit_ref, out_ref):
    x = logit_ref[...]
    e = jnp.exp(x - jnp.max(x, axis=0, keepdims=True))
    p = e / jnp.sum(e, axis=0, keepdims=True)
    acc = jnp.zeros_like(p[0:1])
    rows = [acc]
    for layer in range(1, x.shape[0]):
        acc = acc + p[layer:layer + 1]
        rows.append(acc)
    out_ref[...] = jnp.concatenate(rows, axis=0)


def _lower_bounds(logits):
    return pl.pallas_call(
        _lower_bounds_kernel,
        out_shape=jax.ShapeDtypeStruct(logits.shape, F32),
        name="lower_bounds",
    )(logits.astype(F32))


def _rms(x, gain):
    return x * lax.rsqrt(jnp.mean(x * x, axis=-1, keepdims=True) + RMS_EPS) * gain


def _norm_kernel(x_ref, g_ref, o_ref):
    o_ref[...] = _rms(x_ref[...], g_ref[...]).astype(o_ref.dtype)


def _norm(x, gain, out_dtype, bm=256):
    m, d = x.shape
    bm = _tile(m, bm)
    row = pl.BlockSpec((bm, d), lambda i: (i, 0))
    vec = pl.BlockSpec((1, d), lambda i: (0, 0))
    return pl.pallas_call(
        _norm_kernel, grid=(m // bm,), in_specs=[row, vec], out_specs=row,
        out_shape=jax.ShapeDtypeStruct((m, d), out_dtype),
        compiler_params=_cparams("parallel"), name="rms_norm",
    )(x, gain.reshape(1, d))


def _resid_norm_kernel(h_ref, m_ref, gpost_ref, gpre_ref, h_out_ref, a_out_ref):
    h = h_ref[...] + _rms(m_ref[...].astype(F32), gpost_ref[...])
    h_out_ref[...] = h
    a_out_ref[...] = _rms(h, gpre_ref[...]).astype(a_out_ref.dtype)


def _resid_kernel(h_ref, m_ref, gpost_ref, h_out_ref):
    h_out_ref[...] = h_ref[...] + _rms(m_ref[...].astype(F32), gpost_ref[...])


def _resid_norm(h, m_out, g_post, g_pre, a_dtype, bm=256):
    m, d = h.shape
    bm = _tile(m, bm)
    row = pl.BlockSpec((bm, d), lambda i: (i, 0))
    vec = pl.BlockSpec((1, d), lambda i: (0, 0))
    if g_pre is None:
        return pl.pallas_call(
            _resid_kernel, grid=(m // bm,), in_specs=[row, row, vec], out_specs=row,
            out_shape=jax.ShapeDtypeStruct((m, d), F32),
            compiler_params=_cparams("parallel"), name="resid",
        )(h, m_out, g_post.reshape(1, d)), None
    return pl.pallas_call(
        _resid_norm_kernel, grid=(m // bm,), in_specs=[row, row, vec, vec],
        out_specs=[row, row],
        out_shape=[jax.ShapeDtypeStruct((m, d), F32), jax.ShapeDtypeStruct((m, d), a_dtype)],
        compiler_params=_cparams("parallel"), name="resid_norm",
    )(h, m_out, g_post.reshape(1, d), g_pre.reshape(1, d))


def _mm_kernel(*refs, n_w, n_vec, epilogue, m_split):
    x_ref = refs[0]
    w_refs = refs[1:1 + n_w]
    vec_refs = refs[1 + n_w:1 + n_w + n_vec]
    out_refs = refs[1 + n_w + n_vec:]
    ws = [w[...].astype(BF16) for w in w_refs]
    vecs = [v[...] for v in vec_refs]
    part = x_ref.shape[0] // m_split
    for p in range(m_split):
        rows = slice(p * part, (p + 1) * part)
        accs = [jnp.dot(x_ref[rows, :], w, preferred_element_type=F32) for w in ws]
        outs = epilogue(*accs, *vecs)
        for o_ref, val in zip(out_refs, outs):
            o_ref[rows, :] = val.astype(o_ref.dtype)


def _matmul(x, w, layer, col_starts, n_cols, epilogue, out_dtypes, vecs=(), bm=1024, bn=256, x_buffers=2,
            m_split=1, name="matmul"):
    m, kdim = x.shape
    bm = _tile(m, bm)
    bn = _tile(n_cols, bn)
    x_spec = pl.BlockSpec((bm, kdim), lambda i, j: (i, 0), pipeline_mode=pl.Buffered(x_buffers))
    w_specs = []
    for s in col_starts:
        assert s % bn == 0
        w_specs.append(pl.BlockSpec((None, kdim, bn),
                                    functools.partial(lambda i, j, off: (layer, 0, j + off), off=s // bn)))
    vec_spec = pl.BlockSpec((1, bn), lambda i, j: (0, j))
    out_spec = pl.BlockSpec((bm, bn), lambda i, j: (i, j))
    outs = pl.pallas_call(
        functools.partial(_mm_kernel, n_w=len(col_starts), n_vec=len(vecs), epilogue=epilogue, m_split=m_split),
        grid=(m // bm, n_cols // bn),
        in_specs=[x_spec] + w_specs + [vec_spec] * len(vecs),
        out_specs=[out_spec] * len(out_dtypes),
        out_shape=[jax.ShapeDtypeStruct((m, n_cols), dt) for dt in out_dtypes],
        compiler_params=_cparams("parallel", "arbitrary"), name=name,
    )(x, *([w] * len(col_starts)), *[v.reshape(1, n_cols) for v in vecs])
    return outs


def _mm_rows_kernel(*refs, n_rin, n_rvec, n_w, n_vec, n_rout, row_fn, epilogue, n_tiles, n_sub, rblk):
    rin = refs[:n_rin]
    rvec = refs[n_rin:n_rin + n_rvec]
    w_refs = refs[n_rin + n_rvec:n_rin + n_rvec + n_w]
    vec_refs = refs[n_rin + n_rvec + n_w:n_rin + n_rvec + n_w + n_vec]
    outs_at = n_rin + n_rvec + n_w + n_vec
    rout = refs[outs_at:outs_at + n_rout]
    out_refs = refs[outs_at + n_rout:-2]
    x_even_ref, x_odd_ref = refs[-2:]
    i = pl.program_id(0)
    j = pl.program_id(1)

    def step(x_read_ref, x_write_ref):
        vals = row_fn(*[r[...] for r in rin], *[v[...] for v in rvec])
        start = pl.multiple_of(jnp.minimum(j, n_sub - 1) * rblk, rblk)
        x_write_ref[pl.ds(start, rblk), :] = vals[0].astype(x_write_ref.dtype)
        for o_ref, val in zip(rout, vals[1:]):
            o_ref[...] = val.astype(o_ref.dtype)
        if x_read_ref is not None:
            x = x_read_ref[...]
            accs = [jnp.dot(x, w[...].astype(BF16), preferred_element_type=F32) for w in w_refs]
            outs = epilogue(*accs, *[v[...] for v in vec_refs])
            for o_ref, val in zip(out_refs, outs):
                o_ref[...] = val.astype(o_ref.dtype)

    @pl.when(i == 0)
    def _():
        step(None, x_even_ref)

    @pl.when((i > 0) & (i % 2 == 0))
    def _():
        step(x_odd_ref, x_even_ref)

    @pl.when(i % 2 == 1)
    def _():
        step(x_even_ref, x_odd_ref)


def _matmul_rows(row_fn, row_ins, row_vecs, row_out_dtypes, w, layer, col_starts, n_cols, epilogue, out_dtypes,
                 vecs=(), bm=1024, bn=256, name="matmul_rows"):
    m, kdim = row_ins[0].shape
    bm = _tile(m, bm)
    bn = _tile(n_cols, bn)
    n_tiles, n_j = m // bm, n_cols // bn
    n_sub = 1
    while n_sub * 2 <= n_j and bm % (n_sub * 2) == 0 and (bm // (n_sub * 2)) % 16 == 0:
        n_sub *= 2
    rblk = bm // n_sub
    last = n_tiles * n_sub - 1

    def row_idx(i, j):
        return (jnp.minimum(i * n_sub + jnp.minimum(j, n_sub - 1), last), 0)

    def col(i, j):
        return j * jnp.minimum(i, 1)

    row_spec = pl.BlockSpec((rblk, kdim), row_idx)
    rvec_spec = pl.BlockSpec((1, kdim), lambda i, j: (0, 0))
    w_specs = []
    for s in col_starts:
        assert s % bn == 0
        w_specs.append(pl.BlockSpec((None, kdim, bn),
                                    functools.partial(lambda i, j, off: (layer, 0, col(i, j) + off), off=s // bn)))
    vec_spec = pl.BlockSpec((1, bn), lambda i, j: (0, col(i, j)))
    out_spec = pl.BlockSpec((bm, bn), lambda i, j: (jnp.maximum(i - 1, 0), col(i, j)))
    outs = pl.pallas_call(
        functools.partial(_mm_rows_kernel, n_rin=len(row_ins), n_rvec=len(row_vecs), n_w=len(col_starts),
                          n_vec=len(vecs), n_rout=len(row_out_dtypes), row_fn=row_fn, epilogue=epilogue,
                          n_tiles=n_tiles, n_sub=n_sub, rblk=rblk),
        grid=(n_tiles + 1, n_j),
        in_specs=[row_spec] * len(row_ins) + [rvec_spec] * len(row_vecs) + w_specs + [vec_spec] * len(vecs),
        out_specs=[row_spec] * len(row_out_dtypes) + [out_spec] * len(out_dtypes),
        out_shape=([jax.ShapeDtypeStruct((m, kdim), dt) for dt in row_out_dtypes]
                   + [jax.ShapeDtypeStruct((m, n_cols), dt) for dt in out_dtypes]),
        scratch_shapes=[pltpu.VMEM((bm, kdim), BF16), pltpu.VMEM((bm, kdim), BF16)],
        compiler_params=_cparams("arbitrary", "arbitrary"), name=name,
    )(*row_ins, *[v.reshape(1, kdim) for v in row_vecs], *([w] * len(col_starts)),
      *[v.reshape(1, n_cols) for v in vecs])
    return outs


def _rows_resid_norm(h, m_out, g_post, g_pre):
    h_new = h + _rms(m_out.astype(F32), g_post)
    return _rms(h_new, g_pre), h_new


def _rows_resid_norm_keep(h, m_out, g_post, g_pre):
    a, h_new = _rows_resid_norm(h, m_out, g_post, g_pre)
    return a, h_new, a


def _rows_s5_act(y, a, d_skip):
    return (jax.nn.gelu(y + d_skip * a),)


def _ep_identity(acc):
    return (acc,)


def _ep_silu(acc):
    return (acc * jax.nn.sigmoid(acc),)


def _ep_glu(val, gate):
    return (val * jax.nn.sigmoid(gate),)


def _ep_swiglu(gate, up):
    return (gate * jax.nn.sigmoid(gate) * up,)


def _ep_forget(f, lb):
    e = jnp.exp(-jnp.abs(f))
    one_e = 1.0 + e
    log_sig = jnp.minimum(f, 0.0) - jnp.log(one_e)
    a = jnp.log(lb)
    b = jnp.log1p(-lb) + log_sig
    log_forget = jnp.maximum(a, b) + jnp.log(1.0 + jnp.exp(-jnp.abs(a - b)))
    key = (1.0 - lb) * jnp.where(f >= 0.0, e, 1.0) / one_e
    return log_forget, key


def _s5_prep_kernel(are_ref, aim_ref, ldt_ref, btr_ref, bti_ref, cre_ref, cim_ref,
                    dt_ref, bac_ref, cpc_ref, al_ref):
    gb = are_ref.shape[0]
    h, big_l = S5_GROUP, S5_CHUNK
    lane = lax.broadcasted_iota(jnp.int32, (h, LANES), 1)
    for g in range(gb):
        rows = slice(g * h, (g + 1) * h)
        lam_re = jnp.minimum(are_ref[g], -S5_EIG_CLIP)
        lam_im = aim_ref[g]
        dt = jnp.exp(ldt_ref[g])
        mag = jnp.exp(lam_re * dt)
        ar = mag * jnp.cos(lam_im * dt)
        ai = mag * jnp.sin(lam_im * dt)
        den = lam_re * lam_re + lam_im * lam_im
        z_re = ((ar - 1.0) * lam_re + ai * lam_im) / den
        z_im = (ai * lam_re - (ar - 1.0) * lam_im) / den
        btr, bti = btr_ref[g], bti_ref[g]
        bbr = z_re * btr - z_im * bti
        bbi = z_re * bti + z_im * btr
        cre, cim = cre_ref[g], cim_ref[g]
        pr = jnp.ones_like(ar)
        pi = jnp.zeros_like(ar)
        cp_rows = []
        for tau in range(big_l + 1):
            cp = jnp.concatenate([cre * pr - cim * pi, -(cre * pi + cim * pr)], axis=-1)
            if tau < big_l:
                cp_rows.append(cp)
                ba = jnp.concatenate([bbr * pr - bbi * pi, bbr * pi + bbi * pr], axis=-1)
                bac_ref[tau, rows, :] = ba.astype(bac_ref.dtype)
            if tau >= 1:
                cpc_ref[tau - 1, rows, :] = cp.astype(cpc_ref.dtype)
            if tau == big_l:
                al_ref[0, g:g + 1, :] = jnp.concatenate([pr, pr], axis=-1)
                al_ref[1, g:g + 1, :] = jnp.concatenate([-pi, pi], axis=-1)
            pr, pi = pr * ar - pi * ai, pr * ai + pi * ar
        cp_all = jnp.concatenate(cp_rows, axis=0)
        bb = jnp.concatenate([bbr, bbi], axis=-1)
        krow = lax.dot_general(bb, cp_all, (((1,), (1,)), ((), ())), precision=HIGHEST,
                               preferred_element_type=F32)
        own = (lane >= g * h) & (lane < (g + 1) * h)
        taus_per_vreg = LANES // h
        for tau in range(big_l):
            src = krow[:, (tau // taus_per_vreg) * LANES:(tau // taus_per_vreg + 1) * LANES]
            shift = ((g - tau % taus_per_vreg) * h) % LANES
            moved = src if shift == 0 else pltpu.roll(src, shift, axis=1)
            dt_ref[tau, rows, :] = jnp.where(own, moved, 0.0).astype(dt_ref.dtype)


def _s5_prep(a_re, a_im, log_dt, b_re, b_im, c_re, c_im):
    g, p = a_re.shape
    h, big_l = S5_GROUP, S5_CHUNK
    gb = LANES // h
    assert 2 * p == LANES and g % gb == 0
    row = lambda z: z.astype(F32).reshape(g, 1, p)
    ldt = jnp.broadcast_to(log_dt.astype(F32)[:, None, None], (g, 1, p))
    bt = lambda z: jnp.swapaxes(z.astype(F32), 1, 2)
    vspec = pl.BlockSpec((gb, 1, p), lambda i: (i, 0, 0))
    mspec = pl.BlockSpec((gb, h, p), lambda i: (i, 0, 0))
    op_spec = pl.BlockSpec((None, big_l, LANES, LANES), lambda i: (i, 0, 0, 0))
    op_shape = jax.ShapeDtypeStruct((g // gb, big_l, LANES, LANES), BF16)
    return pl.pallas_call(
        _s5_prep_kernel, grid=(g // gb,),
        in_specs=[vspec, vspec, vspec, mspec, mspec, mspec, mspec],
        out_specs=[op_spec, op_spec, op_spec, pl.BlockSpec((None, 2, gb, LANES), lambda i: (i, 0, 0, 0))],
        out_shape=[op_shape, op_shape, op_shape, jax.ShapeDtypeStruct((g // gb, 2, gb, LANES), F32)],
        compiler_params=_cparams("parallel"), name="s5_prep",
    )(row(a_re), row(a_im), ldt, bt(b_re), bt(b_im), c_re.astype(F32), c_im.astype(F32))


def _s5_main_kernel(a_ref, dt_ref, bac_ref, cpc_ref, al_ref, y_ref,
                    u2_ref, bt_ref, bdw_ref, bdo_ref, w3_ref, w3s_ref, sp3_ref, *, batch, n_chunks):
    h, big_l = S5_GROUP, S5_CHUNK
    gb = LANES // h
    nrows = batch * n_chunks
    nseq = batch * gb

    @pl.when(pl.program_id(0) == 0)
    def _():
        bt_ref[...] = jnp.zeros_like(bt_ref)
        bdw_ref[...] = jnp.zeros_like(bdw_ref)
        bdo_ref[...] = jnp.zeros_like(bdo_ref)

    for s in range(big_l):
        for t in range(s, big_l):
            bt_ref[s * LANES:(s + 1) * LANES, t * LANES:(t + 1) * LANES] = dt_ref[t - s]
        for g in range(gb):
            rows = slice(s * LANES + g * h, s * LANES + (g + 1) * h)
            cols = slice(g * LANES, (g + 1) * LANES)
            bdw_ref[rows, cols] = bac_ref[big_l - 1 - s, g * h:(g + 1) * h, :]
            bdo_ref[rows, cols] = cpc_ref[s, g * h:(g + 1) * h, :]

    for s in range(big_l):
        u2_ref[:, s * LANES:(s + 1) * LANES] = a_ref[pl.ds(s, nrows, stride=big_l), :].astype(BF16)
    u2 = u2_ref[...]

    win = jnp.dot(u2, bdw_ref[...], preferred_element_type=F32)
    for b in range(batch):
        for g in range(gb):
            w3_ref[pl.ds(b * gb + g, n_chunks, stride=nseq), :] = (
                win[b * n_chunks:(b + 1) * n_chunks, g * LANES:(g + 1) * LANES])
    w3s_ref[...] = pltpu.roll(w3_ref[...], LANES // 2, axis=1)
    a1 = jnp.concatenate([al_ref[0]] * batch, axis=0)
    a2 = jnp.concatenate([al_ref[1]] * batch, axis=0)

    def step(c, carry):
        st, st_sw = carry
        base = pl.multiple_of(c * nseq, nseq)
        sp3_ref[pl.ds(base, nseq), :] = st
        new = a1 * st + a2 * st_sw + w3_ref[pl.ds(base, nseq), :]
        new_sw = a1 * st_sw - a2 * st + w3s_ref[pl.ds(base, nseq), :]
        return new, new_sw

    zero = jnp.zeros((nseq, LANES), F32)
    lax.fori_loop(0, n_chunks, step, (zero, zero), unroll=True)
    sprev =jnp.concatenate(
        [jnp.concatenate([sp3_ref[pl.ds(b * gb + g, n_chunks, stride=nseq), :] for b in range(batch)], axis=0)
         for g in range(gb)], axis=1).astype(BF16)

    nt = (((1,), (1,)), ((), ()))
    pair = 2 * LANES
    for tp in range(big_l * LANES // pair):
        cols = slice(tp * pair, (tp + 1) * pair)
        live = (tp + 1) * pair
        y2 = (jnp.dot(u2_ref[:, 0:live], bt_ref[0:live, cols], preferred_element_type=F32)
              + lax.dot_general(sprev, bdo_ref[cols, :], nt, preferred_element_type=F32))
        y_ref[pl.ds(2 * tp, nrows, stride=big_l), :] = y2[:, 0:LANES]
        y_ref[pl.ds(2 * tp + 1, nrows, stride=big_l), :] = y2[:, LANES:pair]


def _s5_core(a, batch, dt_op, bac, cpc, al):
    m, d = a.shape
    t = m // batch
    big_l = S5_CHUNK
    gb = LANES // S5_GROUP
    n_chunks = t // big_l
    nrows = batch * n_chunks
    width = big_l * LANES
    tok = pl.BlockSpec((m, LANES), lambda i: (0, i))
    op_spec = pl.BlockSpec((None, big_l, LANES, LANES), lambda i: (i, 0, 0, 0))
    return pl.pallas_call(
        functools.partial(_s5_main_kernel, batch=batch, n_chunks=n_chunks),
        grid=(d // LANES,),
        in_specs=[tok, op_spec, op_spec, op_spec, pl.BlockSpec((None, 2, gb, LANES), lambda i: (i, 0, 0, 0))],
        out_specs=tok,
        out_shape=jax.ShapeDtypeStruct((m, d), F32),
        scratch_shapes=[pltpu.VMEM((nrows, width), BF16),
                        pltpu.VMEM((width, width), BF16),
                        pltpu.VMEM((width, gb * LANES), BF16),
                        pltpu.VMEM((width, gb * LANES), BF16),
                        pltpu.VMEM((n_chunks * batch * gb, LANES), F32),
                        pltpu.VMEM((n_chunks * batch * gb, LANES), F32),
                        pltpu.VMEM((n_chunks * batch * gb, LANES), F32)],
        compiler_params=_cparams("arbitrary"), name="s5_main",
    )(a, dt_op, bac, cpc, al)


def _hgrn_stage1(q, k, cum, crow, vb, st, diag_mask, with_diag):
    nt = (((1,), (1,)), ((), ()))
    tn = (((0,), (0,)), ((), ()))
    r15, r31, r47, r63 = crow(15), crow(31), crow(47), crow(63)

    def zrows(n):
        return jnp.zeros((n, HGRN_HEAD), F32)

    q0 = q * jnp.exp(cum)
    q15 = q[16:32] * jnp.exp(cum[16:32] - r15)
    q31 = q[32:64] * jnp.exp(cum[32:64] - r31)
    q47 = q[48:64] * jnp.exp(cum[48:64] - r47)
    k63 = k * jnp.exp(r63 - cum)
    k31 = k[0:32] * jnp.exp(r31 - cum[0:32])
    k15 = k[0:16] * jnp.exp(r15 - cum[0:16])
    k47 = k[32:48] * jnp.exp(r47 - cum[32:48])
    o_state = lax.dot_general(q0.astype(BF16), st.astype(BF16), nt, preferred_element_type=F32)
    st_new = st * jnp.exp(r63) + lax.dot_general(vb, k63.astype(BF16), tn, preferred_element_type=F32)
    q_cat = jnp.concatenate([
        jnp.concatenate([zrows(32), q31], axis=0),
        jnp.concatenate([zrows(16), q15, zrows(32)], axis=0),
        jnp.concatenate([zrows(48), q47], axis=0)], axis=-1).astype(BF16)
    k_cat = jnp.concatenate([
        jnp.concatenate([k31, zrows(32)], axis=0),
        jnp.concatenate([k15, zrows(48)], axis=0),
        jnp.concatenate([zrows(32), k47, zrows(16)], axis=0)], axis=-1).astype(BF16)
    scores = lax.dot_general(q_cat, k_cat, nt, preferred_element_type=F32)
    if with_diag:
        q_d = jnp.concatenate([q0[0:16], q15, q31[0:16], q47], axis=0).astype(BF16)
        k_d = jnp.concatenate([
            k[0:16] * jnp.exp(jnp.minimum(-cum[0:16], HGRN_SAFE_DECAY)),
            k[16:32] * jnp.exp(jnp.minimum(r15 - cum[16:32], HGRN_SAFE_DECAY)),
            k[32:48] * jnp.exp(jnp.minimum(r31 - cum[32:48], HGRN_SAFE_DECAY)),
            k[48:64] * jnp.exp(jnp.minimum(r47 - cum[48:64], HGRN_SAFE_DECAY))], axis=0).astype(BF16)
        scores = scores + jnp.where(diag_mask, lax.dot_general(q_d, k_d, nt, preferred_element_type=F32), 0.0)
    return o_state, st_new, scores


def _hgrn_exact_diag(qtile, krow, crow):
    c_len, sub = HGRN_CHUNK, HGRN_SUB
    row8 = lax.broadcasted_iota(jnp.int32, (SUBLANES, HGRN_HEAD), 0)
    lane8 = lax.broadcasted_iota(jnp.int32, (SUBLANES, HGRN_HEAD), 1)
    pairs = [(t0, list(range((t0 // sub) * sub, t0 + SUBLANES))) for t0 in range(0, c_len, SUBLANES)]
    e_tiles = []
    for t0, s_list in pairs:
        qt = qtile(t0)
        ct = jnp.concatenate([crow(t0 + i) for i in range(SUBLANES)], axis=0)
        for s in s_list:
            e = qt * krow(s) * jnp.exp(ct - crow(s))
            if s >= t0:
                e = jnp.where(row8 >= s - t0, e, 0.0)
            e_tiles.append(e)
    e_all = jnp.concatenate(e_tiles, axis=0).astype(BF16)
    sums = jnp.dot(e_all, jnp.ones((HGRN_HEAD, HGRN_HEAD), BF16), preferred_element_type=F32)
    diag_tiles = []
    idx = 0
    for t0, s_list in pairs:
        acc = jnp.zeros((SUBLANES, HGRN_HEAD), F32)
        for s in s_list:
            acc = jnp.where(lane8 == s, sums[idx * SUBLANES:(idx + 1) * SUBLANES], acc)
            idx += 1
        diag_tiles.append(acc)
    return jnp.concatenate(diag_tiles, axis=0)[:, 0:c_len]


def _hgrn_finish(o, scores, vb, gn, sg):
    o = o + jnp.dot(scores.astype(BF16), vb, preferred_element_type=F32)
    o = o * lax.rsqrt(jnp.mean(o * o, axis=-1, keepdims=True) + RMS_EPS)
    return o * gn * sg.astype(F32)


def _hgrn_core_kernel(q_ref, k_ref, lf_ref, v_ref, sg_ref, gn_ref, o_ref, st_ref, cum_ref, kc_ref, cc_ref,
                      tri_ref, *, heads):
    c_len, sub, hd_dim = HGRN_CHUNK, HGRN_SUB, HGRN_HEAD
    tb = q_ref.shape[0]
    n_chunks = tb // c_len
    assert c_len == 4 * sub

    ri = lax.broadcasted_iota(jnp.int32, (c_len, c_len), 0)
    ci = lax.broadcasted_iota(jnp.int32, (c_len, c_len), 1)
    diag_mask = (ri >= ci) & ((ri // sub) == (ci // sub))
    lanes = [slice(hd * hd_dim, (hd + 1) * hd_dim) for hd in range(heads)]

    @pl.when(pl.program_id(2) == 0)
    def _():
        st_ref[...] = jnp.zeros_like(st_ref)
        rb = lax.broadcasted_iota(jnp.int32, (tb, tb), 0)
        cb = lax.broadcasted_iota(jnp.int32, (tb, tb), 1)
        tri = ((rb >= cb) & ((rb // c_len) == (cb // c_len))).astype(BF16)
        tri_ref[...] = jnp.concatenate([tri, tri, tri], axis=1)

    lf = lf_ref[...]
    lf_hi = lf.astype(BF16)
    rest = lf - lf_hi.astype(F32)
    lf_mid = rest.astype(BF16)
    lf_lo = (rest - lf_mid.astype(F32)).astype(BF16)
    cum_ref[...] = jnp.dot(tri_ref[...], jnp.concatenate([lf_hi, lf_mid, lf_lo], axis=0),
                           preferred_element_type=F32)
    block_decay = -jnp.sum(lf.reshape(tb // sub, sub, lf.shape[1]), axis=1)
    exact_needed = jnp.max(block_decay) > HGRN_SAFE_DECAY

    @pl.when(jnp.logical_not(exact_needed))
    def _():
        for c in range(n_chunks):
            r0 = c * c_len
            rows = slice(r0, r0 + c_len)
            for hd, ls in enumerate(lanes):
                vb = v_ref[rows, ls].astype(BF16)
                o, st_new, scores = _hgrn_stage1(
                    q_ref[rows, ls], k_ref[rows, ls], cum_ref[rows, ls],
                    lambda i, r0=r0, ls=ls: cum_ref[r0 + i:r0 + i + 1, ls],
                    vb, st_ref[hd], diag_mask, True)
                st_ref[hd] = st_new
                o_ref[rows, ls] = _hgrn_finish(o, scores, vb, gn_ref[:, ls], sg_ref[rows, ls]).astype(o_ref.dtype)

    @pl.when(exact_needed)
    def _():
        def chunk_body(c, carry):
            r0 = pl.multiple_of(c * c_len, c_len)
            rows = pl.ds(r0, c_len)
            cc_ref[...] = cum_ref[rows, :]
            kc_ref[...] = k_ref[rows, :]
            for hd, ls in enumerate(lanes):
                vb = v_ref[rows, ls].astype(BF16)
                o, st_new, scores = _hgrn_stage1(
                    q_ref[rows, ls], kc_ref[:, ls], cc_ref[:, ls], lambda i, ls=ls: cc_ref[i:i + 1, ls],
                    vb, st_ref[hd], diag_mask, False)
                st_ref[hd] = st_new
                scores = scores + _hgrn_exact_diag(
                    lambda t0, ls=ls: q_ref[pl.ds(r0 + t0, SUBLANES), ls],
                    lambda s, ls=ls: kc_ref[s:s + 1, ls],
                    lambda s, ls=ls: cc_ref[s:s + 1, ls])
                o_ref[rows, ls] = _hgrn_finish(o, scores, vb, gn_ref[:, ls], sg_ref[rows, ls]).astype(o_ref.dtype)
            return carry

        lax.fori_loop(0, n_chunks, chunk_body, 0)


def _hgrn_core(q, k, log_f, v, sg, g_norm, batch, tb=256, heads=4):
    m, d = q.shape
    t = m // batch
    tb = _tile(t, tb)
    heads = _tile(d // HGRN_HEAD, heads)
    wl = heads * HGRN_HEAD
    nt = t // tb
    blk = pl.BlockSpec((tb, wl), lambda b, j, i: (b * nt + i, j))
    vec = pl.BlockSpec((1, wl), lambda b, j, i: (0, j))
    return pl.pallas_call(
        functools.partial(_hgrn_core_kernel, heads=heads),
        grid=(batch, d // wl, nt),
        in_specs=[blk, blk, blk, blk, blk, vec],
        out_specs=blk,
        out_shape=jax.ShapeDtypeStruct((m, d), BF16),
        scratch_shapes=[pltpu.VMEM((heads, HGRN_HEAD, HGRN_HEAD), F32),
                        pltpu.VMEM((tb, wl), F32),
                        pltpu.VMEM((HGRN_CHUNK, wl), F32),
                        pltpu.VMEM((HGRN_CHUNK, wl), F32),
                        pltpu.VMEM((tb, 3 * tb), BF16)],
        compiler_params=_cparams("parallel", "parallel", "arbitrary"), name="hgrn_core",
    )(q, k, log_f, v, sg, g_norm.astype(F32).reshape(1, d))


def kernel(x, norm_gains, s5_a_re, s5_a_im, s5_log_dt, s5_b_re, s5_b_im, s5_c_re, s5_c_im, s5_d,
           s5_w_glu, hgrn_w_in, hgrn_lb_logits, hgrn_g_norm, hgrn_w_out, ffn_w_gate_up, ffn_w_down):
    batch, seq, d = x.shape
    depth = norm_gains.shape[0]
    d_ff = ffn_w_down.shape[1]
    m = batch * seq
    gains = norm_gains.astype(F32)
    lower_bounds = _lower_bounds(hgrn_lb_logits)

    h = x.reshape(m, d).astype(F32)
    a = _norm(h, gains[0, 0], F32)
    f_out = None
    for layer in range(depth):
        j = layer // 2
        if layer % 2 == 0:
            if f_out is not None:
                h, a = _resid_norm(h, f_out, gains[layer - 1, 3], gains[layer, 0], F32)
            s5_ops = _s5_prep(s5_a_re[j], s5_a_im[j], s5_log_dt[j], s5_b_re[j], s5_b_im[j],
                              s5_c_re[j], s5_c_im[j])
            y = _s5_core(a, batch, *s5_ops)
            (mix,) = _matmul_rows(_rows_s5_act, (y, a), (s5_d[j].astype(F32),), (), s5_w_glu, j, (0, d), d,
                                  _ep_glu, (BF16,), name="s5_glu")
        else:
            h, a, q = _matmul_rows(_rows_resid_norm_keep, (h, f_out), (gains[layer - 1, 3], gains[layer, 0]),
                                   (F32, BF16), hgrn_w_in, j, (0,), d, _ep_silu, (F32,), bn=512, name="hgrn_q")
            log_f, key = _matmul(a, hgrn_w_in, j, (d,), d, _ep_forget, (F32, F32), vecs=(lower_bounds[layer],),
                                 bm=2048, m_split=2, name="hgrn_f")
            (val,) = _matmul(a, hgrn_w_in, j, (2 * d,), d, _ep_identity, (F32,), bn=512, name="hgrn_v")
            (sg,) = _matmul(a, hgrn_w_in, j, (3 * d,), d, _ep_silu, (BF16,), bn=512, name="hgrn_g")
            o = _hgrn_core(q, key, log_f, val, sg, hgrn_g_norm[j], batch)
            (mix,) = _matmul(o, hgrn_w_out, j, (0,), d, _ep_identity, (BF16,), bn=512, name="hgrn_out")
        h, act = _matmul_rows(_rows_resid_norm, (h, mix), (gains[layer, 1], gains[layer, 2]), (F32,),
                              ffn_w_gate_up, layer, (0, d_ff), d_ff, _ep_swiglu, (BF16,), name="ffn_up")
        (f_out,) = _matmul(act, ffn_w_down, layer, (0,), d, _ep_identity, (BF16,), x_buffers=1, name="ffn_down")
    h, _ = _resid_norm(h, f_out, gains[depth - 1, 3], None, None)
    return h.reshape(batch, seq, d).astype(x.dtype)
```

```python
import functools

import jax
import jax.numpy as jnp
from jax import lax
from jax.experimental import pallas as pl
from jax.experimental.pallas import tpu as pltpu

F32 = jnp.float32
BF16 = jnp.bfloat16
HIGHEST = lax.Precision.HIGHEST

RMS_EPS = 1e-6
S5_GROUP = 16
S5_STATE = 64
S5_EIG_CLIP = 1e-4
S5_CHUNK = 16
HGRN_HEAD = 128
HGRN_CHUNK = 64
HGRN_SUB = 16
HGRN_SAFE_DECAY = 60.0
LANES = 128
SUBLANES = 8
VMEM_LIMIT = 56 * 1024 * 1024


def _cparams(*sem):
    return pltpu.CompilerParams(dimension_semantics=sem, vmem_limit_bytes=VMEM_LIMIT)


def _tile(n, want):
    t = min(n, want)
    assert n % t == 0, (n, want)
    return t


def _lower_bounds_kernel(logit_ref, out_ref):
    x = logit_ref[...]
    e = jnp.exp(x - jnp.max(x, axis=0, keepdims=True))
    p = e / jnp.sum(e, axis=0, keepdims=True)
    acc = jnp.zeros_like(p[0:1])
    rows = [acc]
    for layer in range(1, x.shape[0]):
        acc = acc + p[layer:layer + 1]
        rows.append(acc)
    out_ref[...] = jnp.concatenate(rows, axis=0)


def _lower_bounds(logits):
    return pl.pallas_call(
        _lower_bounds_kernel,
        out_shape=jax.ShapeDtypeStruct(logits.shape, F32),
        name="lower_bounds",
    )(logits.astype(F32))


def _rms(x, gain):
    return x * lax.rsqrt(jnp.mean(x * x, axis=-1, keepdims=True) + RMS_EPS) * gain


def _norm_kernel(x_ref, g_ref, o_ref):
    o_ref[...] = _rms(x_ref[...], g_ref[...]).astype(o_ref.dtype)


def _norm(x, gain, out_dtype, bm=256):
    m, d = x.shape
    bm = _tile(m, bm)
    row = pl.BlockSpec((bm, d), lambda i: (i, 0))
    vec = pl.BlockSpec((1, d), lambda i: (0, 0))
    return pl.pallas_call(
        _norm_kernel, grid=(m // bm,), in_specs=[row, vec], out_specs=row,
        out_shape=jax.ShapeDtypeStruct((m, d), out_dtype),
        compiler_params=_cparams("parallel"), name="rms_norm",
    )(x, gain.reshape(1, d))


def _resid_norm_kernel(h_ref, m_ref, gpost_ref, gpre_ref, h_out_ref, a_out_ref):
    h = h_ref[...] + _rms(m_ref[...].astype(F32), gpost_ref[...])
    h_out_ref[...] = h
    a_out_ref[...] = _rms(h, gpre_ref[...]).astype(a_out_ref.dtype)


def _resid_kernel(h_ref, m_ref, gpost_ref, h_out_ref):
    h_out_ref[...] = h_ref[...] + _rms(m_ref[...].astype(F32), gpost_ref[...])


def _resid_norm(h, m_out, g_post, g_pre, a_dtype, bm=256):
    m, d = h.shape
    bm = _tile(m, bm)
    row = pl.BlockSpec((bm, d), lambda i: (i, 0))
    vec = pl.BlockSpec((1, d), lambda i: (0, 0))
    if g_pre is None:
        return pl.pallas_call(
            _resid_kernel, grid=(m // bm,), in_specs=[row, row, vec], out_specs=row,
            out_shape=jax.ShapeDtypeStruct((m, d), F32),
            compiler_params=_cparams("parallel"), name="resid",
        )(h, m_out, g_post.reshape(1, d)), None
    return pl.pallas_call(
        _resid_norm_kernel, grid=(m // bm,), in_specs=[row, row, vec, vec],
        out_specs=[row, row],
        out_shape=[jax.ShapeDtypeStruct((m, d), F32), jax.ShapeDtypeStruct((m, d), a_dtype)],
        compiler_params=_cparams("parallel"), name="resid_norm",
    )(h, m_out, g_post.reshape(1, d), g_pre.reshape(1, d))


def _mm_kernel(*refs, n_w, n_vec, epilogue, m_split):
    x_ref = refs[0]
    w_refs = refs[1:1 + n_w]
    vec_refs = refs[1 + n_w:1 + n_w + n_vec]
    out_refs = refs[1 + n_w + n_vec:]
    ws = [w[...].astype(BF16) for w in w_refs]
    vecs = [v[...] for v in vec_refs]
    part = x_ref.shape[0] // m_split
    for p in range(m_split):
        rows = slice(p * part, (p + 1) * part)
        accs = [jnp.dot(x_ref[rows, :], w, preferred_element_type=F32) for w in ws]
        outs = epilogue(*accs, *vecs)
        for o_ref, val in zip(out_refs, outs):
            o_ref[rows, :] = val.astype(o_ref.dtype)


def _matmul(x, w, layer, col_starts, n_cols, epilogue, out_dtypes, vecs=(), bm=1024, bn=256, x_buffers=2,
            m_split=1, name="matmul"):
    m, kdim = x.shape
    bm = _tile(m, bm)
    bn = _tile(n_cols, bn)
    x_spec = pl.BlockSpec((bm, kdim), lambda i, j: (i, 0), pipeline_mode=pl.Buffered(x_buffers))
    w_specs = []
    for s in col_starts:
        assert s % bn == 0
        w_specs.append(pl.BlockSpec((None, kdim, bn),
                                    functools.partial(lambda i, j, off: (layer, 0, j + off), off=s // bn)))
    vec_spec = pl.BlockSpec((1, bn), lambda i, j: (0, j))
    out_spec = pl.BlockSpec((bm, bn), lambda i, j: (i, j))
    outs = pl.pallas_call(
        functools.partial(_mm_kernel, n_w=len(col_starts), n_vec=len(vecs), epilogue=epilogue, m_split=m_split),
        grid=(m // bm, n_cols // bn),
        in_specs=[x_spec] + w_specs + [vec_spec] * len(vecs),
        out_specs=[out_spec] * len(out_dtypes),
        out_shape=[jax.ShapeDtypeStruct((m, n_cols), dt) for dt in out_dtypes],
        compiler_params=_cparams("parallel", "arbitrary"), name=name,
    )(x, *([w] * len(col_starts)), *[v.reshape(1, n_cols) for v in vecs])
    return outs


def _mm_rows_kernel(*refs, n_rin, n_rvec, n_w, n_vec, n_rout, row_fn, epilogue, n_tiles, n_sub, rblk):
    rin = refs[:n_rin]
    rvec = refs[n_rin:n_rin + n_rvec]
    w_refs = refs[n_rin + n_rvec:n_rin + n_rvec + n_w]
    vec_refs = refs[n_rin + n_rvec + n_w:n_rin + n_rvec + n_w + n_vec]
    outs_at = n_rin + n_rvec + n_w + n_vec
    rout = refs[outs_at:outs_at + n_rout]
    out_refs = refs[outs_at + n_rout:-2]
    x_even_ref, x_odd_ref = refs[-2:]
    i = pl.program_id(0)
    j = pl.program_id(1)

    def step(x_read_ref, x_write_ref):
        vals = row_fn(*[r[...] for r in rin], *[v[...] for v in rvec])
        start = pl.multiple_of(jnp.minimum(j, n_sub - 1) * rblk, rblk)
        x_write_ref[pl.ds(start, rblk), :] = vals[0].astype(x_write_ref.dtype)
        for o_ref, val in zip(rout, vals[1:]):
            o_ref[...] = val.astype(o_ref.dtype)
        if x_read_ref is not None:
            x = x_read_ref[...]
            accs = [jnp.dot(x, w[...].astype(BF16), preferred_element_type=F32) for w in w_refs]
            outs = epilogue(*accs, *[v[...] for v in vec_refs])
            for o_ref, val in zip(out_refs, outs):
                o_ref[...] = val.astype(o_ref.dtype)

    @pl.when(i == 0)
    def _():
        step(None, x_even_ref)

    @pl.when((i > 0) & (i % 2 == 0))
    def _():
        step(x_odd_ref, x_even_ref)

    @pl.when(i % 2 == 1)
    def _():
        step(x_even_ref, x_odd_ref)


def _matmul_rows(row_fn, row_ins, row_vecs, row_out_dtypes, w, layer, col_starts, n_cols, epilogue, out_dtypes,
                 vecs=(), bm=1024, bn=256, name="matmul_rows"):
    m, kdim = row_ins[0].shape
    bm = _tile(m, bm)
    bn = _tile(n_cols, bn)
    n_tiles, n_j = m // bm, n_cols // bn
    n_sub = 1
    while n_sub * 2 <= n_j and bm % (n_sub * 2) == 0 and (bm // (n_sub * 2)) % 16 == 0:
        n_sub *= 2
    rblk = bm // n_sub
    last = n_tiles * n_sub - 1

    def row_idx(i, j):
        return (jnp.minimum(i * n_sub + jnp.minimum(j, n_sub - 1), last), 0)

    def col(i, j):
        return j * jnp.minimum(i, 1)

    row_spec = pl.BlockSpec((rblk, kdim), row_idx)
    rvec_spec = pl.BlockSpec((1, kdim), lambda i, j: (0, 0))
    w_specs = []
    for s in col_starts:
        assert s % bn == 0
        w_specs.append(pl.BlockSpec((None, kdim, bn),
                                    functools.partial(lambda i, j, off: (layer, 0, col(i, j) + off), off=s // bn)))
    vec_spec = pl.BlockSpec((1, bn), lambda i, j: (0, col(i, j)))
    out_spec = pl.BlockSpec((bm, bn), lambda i, j: (jnp.maximum(i - 1, 0), col(i, j)))
    outs = pl.pallas_call(
        functools.partial(_mm_rows_kernel, n_rin=len(row_ins), n_rvec=len(row_vecs), n_w=len(col_starts),
                          n_vec=len(vecs), n_rout=len(row_out_dtypes), row_fn=row_fn, epilogue=epilogue,
                          n_tiles=n_tiles, n_sub=n_sub, rblk=rblk),
        grid=(n_tiles + 1, n_j),
        in_specs=[row_spec] * len(row_ins) + [rvec_spec] * len(row_vecs) + w_specs + [vec_spec] * len(vecs),
        out_specs=[row_spec] * len(row_out_dtypes) + [out_spec] * len(out_dtypes),
        out_shape=([jax.ShapeDtypeStruct((m, kdim), dt) for dt in row_out_dtypes]
                   + [jax.ShapeDtypeStruct((m, n_cols), dt) for dt in out_dtypes]),
        scratch_shapes=[pltpu.VMEM((bm, kdim), BF16), pltpu.VMEM((bm, kdim), BF16)],
        compiler_params=_cparams("arbitrary", "arbitrary"), name=name,
    )(*row_ins, *[v.reshape(1, kdim) for v in row_vecs], *([w] * len(col_starts)),
      *[v.reshape(1, n_cols) for v in vecs])
    return outs


def _rows_resid_norm(h, m_out, g_post, g_pre):
    h_new = h + _rms(m_out.astype(F32), g_post)
    a = _rms(h_new, g_pre)
    return a, h_new, a


def _ep_identity(acc):
    return (acc,)


def _ep_silu(acc):
    return (acc * jax.nn.sigmoid(acc),)


def _ep_glu(val, gate):
    return (val * jax.nn.sigmoid(gate),)


def _ep_swiglu(gate, up):
    return (gate * jax.nn.sigmoid(gate) * up,)


def _ep_forget(f, lb):
    e = jnp.exp(-jnp.abs(f))
    one_e = 1.0 + e
    log_sig = jnp.minimum(f, 0.0) - jnp.log(one_e)
    a = jnp.log(lb)
    b = jnp.log1p(-lb) + log_sig
    log_forget = jnp.maximum(a, b) + jnp.log(1.0 + jnp.exp(-jnp.abs(a - b)))
    key = (1.0 - lb) * jnp.where(f >= 0.0, e, 1.0) / one_e
    return log_forget, key


def _s5_prep_kernel(are_ref, aim_ref, ldt_ref, btr_ref, bti_ref, cre_ref, cim_ref,
                    dt_ref, bac_ref, cpc_ref, al_ref):
    gb = are_ref.shape[0]
    h, big_l = S5_GROUP, S5_CHUNK
    lane = lax.broadcasted_iota(jnp.int32, (h, LANES), 1)
    for g in range(gb):
        rows = slice(g * h, (g + 1) * h)
        lam_re = jnp.minimum(are_ref[g], -S5_EIG_CLIP)
        lam_im = aim_ref[g]
        dt = jnp.exp(ldt_ref[g])
        mag = jnp.exp(lam_re * dt)
        ar = mag * jnp.cos(lam_im * dt)
        ai = mag * jnp.sin(lam_im * dt)
        den = lam_re * lam_re + lam_im * lam_im
        z_re = ((ar - 1.0) * lam_re + ai * lam_im) / den
        z_im = (ai * lam_re - (ar - 1.0) * lam_im) / den
        btr, bti = btr_ref[g], bti_ref[g]
        bbr = z_re * btr - z_im * bti
        bbi = z_re * bti + z_im * btr
        cre, cim = cre_ref[g], cim_ref[g]
        pr = jnp.ones_like(ar)
        pi = jnp.zeros_like(ar)
        cp_rows = []
        for tau in range(big_l + 1):
            cp = jnp.concatenate([cre * pr - cim * pi, -(cre * pi + cim * pr)], axis=-1)
            if tau < big_l:
                cp_rows.append(cp)
                ba = jnp.concatenate([bbr * pr - bbi * pi, bbr * pi + bbi * pr], axis=-1)
                bac_ref[tau, rows, :] = ba.astype(bac_ref.dtype)
            if tau >= 1:
                cpc_ref[tau - 1, rows, :] = cp.astype(cpc_ref.dtype)
            if tau == big_l:
                al_ref[0, g:g + 1, :] = jnp.concatenate([pr, pr], axis=-1)
                al_ref[1, g:g + 1, :] = jnp.concatenate([-pi, pi], axis=-1)
            pr, pi = pr * ar - pi * ai, pr * ai + pi * ar
        cp_all = jnp.concatenate(cp_rows, axis=0)
        bb = jnp.concatenate([bbr, bbi], axis=-1)
        krow = lax.dot_general(bb, cp_all, (((1,), (1,)), ((), ())), precision=HIGHEST,
                               preferred_element_type=F32)
        own = (lane >= g * h) & (lane < (g + 1) * h)
        taus_per_vreg = LANES // h
        for tau in range(big_l):
            src = krow[:, (tau // taus_per_vreg) * LANES:(tau // taus_per_vreg + 1) * LANES]
            shift = ((g - tau % taus_per_vreg) * h) % LANES
            moved = src if shift == 0 else pltpu.roll(src, shift, axis=1)
            dt_ref[tau, rows, :] = jnp.where(own, moved, 0.0).astype(dt_ref.dtype)


def _s5_prep(a_re, a_im, log_dt, b_re, b_im, c_re, c_im):
    g, p = a_re.shape
    h, big_l = S5_GROUP, S5_CHUNK
    gb = LANES // h
    assert 2 * p == LANES and g % gb == 0
    row = lambda z: z.astype(F32).reshape(g, 1, p)
    ldt = jnp.broadcast_to(log_dt.astype(F32)[:, None, None], (g, 1, p))
    bt = lambda z: jnp.swapaxes(z.astype(F32), 1, 2)
    vspec = pl.BlockSpec((gb, 1, p), lambda i: (i, 0, 0))
    mspec = pl.BlockSpec((gb, h, p), lambda i: (i, 0, 0))
    op_spec = pl.BlockSpec((None, big_l, LANES, LANES), lambda i: (i, 0, 0, 0))
    op_shape = jax.ShapeDtypeStruct((g // gb, big_l, LANES, LANES), BF16)
    return pl.pallas_call(
        _s5_prep_kernel, grid=(g // gb,),
        in_specs=[vspec, vspec, vspec, mspec, mspec, mspec, mspec],
        out_specs=[op_spec, op_spec, op_spec, pl.BlockSpec((None, 2, gb, LANES), lambda i: (i, 0, 0, 0))],
        out_shape=[op_shape, op_shape, op_shape, jax.ShapeDtypeStruct((g // gb, 2, gb, LANES), F32)],
        compiler_params=_cparams("parallel"), name="s5_prep",
    )(row(a_re), row(a_im), ldt, bt(b_re), bt(b_im), c_re.astype(F32), c_im.astype(F32))


def _s5_main_kernel(a_ref, dt_ref, bac_ref, cpc_ref, al_ref, y_ref,
                    u2_ref, bt_ref, bdw_ref, bdo_ref, w3_ref, w3s_ref, sp3_ref, *, batch, n_chunks):
    h, big_l = S5_GROUP, S5_CHUNK
    gb = LANES // h
    nrows = batch * n_chunks
    nseq = batch * gb

    @pl.when(pl.program_id(0) == 0)
    def _():
        bt_ref[...] = jnp.zeros_like(bt_ref)
        bdw_ref[...] = jnp.zeros_like(bdw_ref)
        bdo_ref[...] = jnp.zeros_like(bdo_ref)

    for s in range(big_l):
        for t in range(s, big_l):
            bt_ref[s * LANES:(s + 1) * LANES, t * LANES:(t + 1) * LANES] = dt_ref[t - s]
        for g in range(gb):
            rows = slice(s * LANES + g * h, s * LANES + (g + 1) * h)
            cols = slice(g * LANES, (g + 1) * LANES)
            bdw_ref[rows, cols] = bac_ref[big_l - 1 - s, g * h:(g + 1) * h, :]
            bdo_ref[rows, cols] = cpc_ref[s, g * h:(g + 1) * h, :]

    for s in range(big_l):
        u2_ref[:, s * LANES:(s + 1) * LANES] = a_ref[pl.ds(s, nrows, stride=big_l), :].astype(BF16)
    u2 = u2_ref[...]

    win = jnp.dot(u2, bdw_ref[...], preferred_element_type=F32)
    for b in range(batch):
        for g in range(gb):
            w3_ref[pl.ds(b * gb + g, n_chunks, stride=nseq), :] = (
                win[b * n_chunks:(b + 1) * n_chunks, g * LANES:(g + 1) * LANES])
    w3s_ref[...] = pltpu.roll(w3_ref[...], LANES // 2, axis=1)
    a1 = jnp.concatenate([al_ref[0]] * batch, axis=0)
    a2 = jnp.concatenate([al_ref[1]] * batch, axis=0)

    def step(c, carry):
        st, st_sw = carry
        base = pl.multiple_of(c * nseq, nseq)
        sp3_ref[pl.ds(base, nseq), :] = st
        new = a1 * st + a2 * st_sw + w3_ref[pl.ds(base, nseq), :]
        new_sw = a1 * st_sw - a2 * st + w3s_ref[pl.ds(base, nseq), :]
        return new, new_sw

    zero = jnp.zeros((nseq, LANES), F32)
    lax.fori_loop(0, n_chunks, step, (zero, zero), unroll=True)
    sprev =jnp.concatenate(
        [jnp.concatenate([sp3_ref[pl.ds(b * gb + g, n_chunks, stride=nseq), :] for b in range(batch)], axis=0)
         for g in range(gb)], axis=1).astype(BF16)

    nt = (((1,), (1,)), ((), ()))
    pair = 2 * LANES
    for tp in range(big_l * LANES // pair):
        cols = slice(tp * pair, (tp + 1) * pair)
        live = (tp + 1) * pair
        y2 = (jnp.dot(u2_ref[:, 0:live], bt_ref[0:live, cols], preferred_element_type=F32)
              + lax.dot_general(sprev, bdo_ref[cols, :], nt, preferred_element_type=F32))
        y_ref[pl.ds(2 * tp, nrows, stride=big_l), :] = y2[:, 0:LANES]
        y_ref[pl.ds(2 * tp + 1, nrows, stride=big_l), :] = y2[:, LANES:pair]


def _s5_core(a, batch, dt_op, bac, cpc, al):
    m, d = a.shape
    t = m // batch
    big_l = S5_CHUNK
    gb = LANES // S5_GROUP
    n_chunks = t // big_l
    nrows = batch * n_chunks
    width = big_l * LANES
    tok = pl.BlockSpec((m, LANES), lambda i: (0, i))
    op_spec = pl.BlockSpec((None, big_l, LANES, LANES), lambda i: (i, 0, 0, 0))
    return pl.pallas_call(
        functools.partial(_s5_main_kernel, batch=batch, n_chunks=n_chunks),
        grid=(d // LANES,),
        in_specs=[tok, op_spec, op_spec, op_spec, pl.BlockSpec((None, 2, gb, LANES), lambda i: (i, 0, 0, 0))],
        out_specs=tok,
        out_shape=jax.ShapeDtypeStruct((m, d), F32),
        scratch_shapes=[pltpu.VMEM((nrows, width), BF16),
                        pltpu.VMEM((width, width), BF16),
                        pltpu.VMEM((width, gb * LANES), BF16),
                        pltpu.VMEM((width, gb * LANES), BF16),
                        pltpu.VMEM((n_chunks * batch * gb, LANES), F32),
                        pltpu.VMEM((n_chunks * batch * gb, LANES), F32),
                        pltpu.VMEM((n_chunks * batch * gb, LANES), F32)],
        compiler_params=_cparams("arbitrary"), name="s5_main",
    )(a, dt_op, bac, cpc, al)


def _s5_act_kernel(y_ref, a_ref, d_ref, z_ref):
    z_ref[...] = jax.nn.gelu(y_ref[...] + d_ref[...] * a_ref[...]).astype(z_ref.dtype)


def _s5_act(y, a, d_skip, bm=256):
    m, d = y.shape
    bm = _tile(m, bm)
    row = pl.BlockSpec((bm, d), lambda i: (i, 0))
    vec = pl.BlockSpec((1, d), lambda i: (0, 0))
    return pl.pallas_call(
        _s5_act_kernel, grid=(m // bm,), in_specs=[row, row, vec], out_specs=row,
        out_shape=jax.ShapeDtypeStruct((m, d), BF16),
        compiler_params=_cparams("parallel"), name="s5_act",
    )(y, a, d_skip.astype(F32).reshape(1, d))


def _hgrn_stage1(q, k, cum, crow, vb, st, diag_mask, with_diag):
    nt = (((1,), (1,)), ((), ()))
    tn = (((0,), (0,)), ((), ()))
    r15, r31, r47, r63 = crow(15), crow(31), crow(47), crow(63)

    def zrows(n):
        return jnp.zeros((n, HGRN_HEAD), F32)

    q0 = q * jnp.exp(cum)
    q15 = q[16:32] * jnp.exp(cum[16:32] - r15)
    q31 = q[32:64] * jnp.exp(cum[32:64] - r31)
    q47 = q[48:64] * jnp.exp(cum[48:64] - r47)
    k63 = k * jnp.exp(r63 - cum)
    k31 = k[0:32] * jnp.exp(r31 - cum[0:32])
    k15 = k[0:16] * jnp.exp(r15 - cum[0:16])
    k47 = k[32:48] * jnp.exp(r47 - cum[32:48])
    o_state = lax.dot_general(q0.astype(BF16), st.astype(BF16), nt, preferred_element_type=F32)
    st_new = st * jnp.exp(r63) + lax.dot_general(vb, k63.astype(BF16), tn, preferred_element_type=F32)
    q_cat = jnp.concatenate([
        jnp.concatenate([zrows(32), q31], axis=0),
        jnp.concatenate([zrows(16), q15, zrows(32)], axis=0),
        jnp.concatenate([zrows(48), q47], axis=0)], axis=-1).astype(BF16)
    k_cat = jnp.concatenate([
        jnp.concatenate([k31, zrows(32)], axis=0),
        jnp.concatenate([k15, zrows(48)], axis=0),
        jnp.concatenate([zrows(32), k47, zrows(16)], axis=0)], axis=-1).astype(BF16)
    scores = lax.dot_general(q_cat, k_cat, nt, preferred_element_type=F32)
    if with_diag:
        q_d = jnp.concatenate([q0[0:16], q15, q31[0:16], q47], axis=0).astype(BF16)
        k_d = jnp.concatenate([
            k[0:16] * jnp.exp(jnp.minimum(-cum[0:16], HGRN_SAFE_DECAY)),
            k[16:32] * jnp.exp(jnp.minimum(r15 - cum[16:32], HGRN_SAFE_DECAY)),
            k[32:48] * jnp.exp(jnp.minimum(r31 - cum[32:48], HGRN_SAFE_DECAY)),
            k[48:64] * jnp.exp(jnp.minimum(r47 - cum[48:64], HGRN_SAFE_DECAY))], axis=0).astype(BF16)
        scores = scores + jnp.where(diag_mask, lax.dot_general(q_d, k_d, nt, preferred_element_type=F32), 0.0)
    return o_state, st_new, scores


def _hgrn_exact_diag(qtile, krow, crow):
    c_len, sub = HGRN_CHUNK, HGRN_SUB
    row8 = lax.broadcasted_iota(jnp.int32, (SUBLANES, HGRN_HEAD), 0)
    lane8 = lax.broadcasted_iota(jnp.int32, (SUBLANES, HGRN_HEAD), 1)
    pairs = [(t0, list(range((t0 // sub) * sub, t0 + SUBLANES))) for t0 in range(0, c_len, SUBLANES)]
    e_tiles = []
    for t0, s_list in pairs:
        qt = qtile(t0)
        ct = jnp.concatenate([crow(t0 + i) for i in range(SUBLANES)], axis=0)
        for s in s_list:
            e = qt * krow(s) * jnp.exp(ct - crow(s))
            if s >= t0:
                e = jnp.where(row8 >= s - t0, e, 0.0)
            e_tiles.append(e)
    e_all = jnp.concatenate(e_tiles, axis=0).astype(BF16)
    sums = jnp.dot(e_all, jnp.ones((HGRN_HEAD, HGRN_HEAD), BF16), preferred_element_type=F32)
    diag_tiles = []
    idx = 0
    for t0, s_list in pairs:
        acc = jnp.zeros((SUBLANES, HGRN_HEAD), F32)
        for s in s_list:
            acc = jnp.where(lane8 == s, sums[idx * SUBLANES:(idx + 1) * SUBLANES], acc)
            idx += 1
        diag_tiles.append(acc)
    return jnp.concatenate(diag_tiles, axis=0)[:, 0:c_len]


def _hgrn_finish(o, scores, vb, gn, sg):
    o = o + jnp.dot(scores.astype(BF16), vb, preferred_element_type=F32)
    o = o * lax.rsqrt(jnp.mean(o * o, axis=-1, keepdims=True) + RMS_EPS)
    return o * gn * sg.astype(F32)


def _hgrn_core_kernel(q_ref, k_ref, lf_ref, v_ref, sg_ref, gn_ref, o_ref, st_ref, cum_ref, kc_ref, cc_ref,
                      tri_ref, *, heads):
    c_len, sub, hd_dim = HGRN_CHUNK, HGRN_SUB, HGRN_HEAD
    tb = q_ref.shape[0]
    n_chunks = tb // c_len
    assert c_len == 4 * sub

    ri = lax.broadcasted_iota(jnp.int32, (c_len, c_len), 0)
    ci = lax.broadcasted_iota(jnp.int32, (c_len, c_len), 1)
    diag_mask = (ri >= ci) & ((ri // sub) == (ci // sub))
    lanes = [slice(hd * hd_dim, (hd + 1) * hd_dim) for hd in range(heads)]

    @pl.when(pl.program_id(2) == 0)
    def _():
        st_ref[...] = jnp.zeros_like(st_ref)
        rb = lax.broadcasted_iota(jnp.int32, (tb, tb), 0)
        cb = lax.broadcasted_iota(jnp.int32, (tb, tb), 1)
        tri = ((rb >= cb) & ((rb // c_len) == (cb // c_len))).astype(BF16)
        tri_ref[...] = jnp.concatenate([tri, tri, tri], axis=1)

    lf = lf_ref[...]
    lf_hi = lf.astype(BF16)
    rest = lf - lf_hi.astype(F32)
    lf_mid = rest.astype(BF16)
    lf_lo = (rest - lf_mid.astype(F32)).astype(BF16)
    cum_ref[...] = jnp.dot(tri_ref[...], jnp.concatenate([lf_hi, lf_mid, lf_lo], axis=0),
                           preferred_element_type=F32)
    block_decay = -jnp.sum(lf.reshape(tb // sub, sub, lf.shape[1]), axis=1)
    exact_needed = jnp.max(block_decay) > HGRN_SAFE_DECAY

    @pl.when(jnp.logical_not(exact_needed))
    def _():
        for c in range(n_chunks):
            r0 = c * c_len
            rows = slice(r0, r0 + c_len)
            for hd, ls in enumerate(lanes):
                vb = v_ref[rows, ls].astype(BF16)
                o, st_new, scores = _hgrn_stage1(
                    q_ref[rows, ls], k_ref[rows, ls], cum_ref[rows, ls],
                    lambda i, r0=r0, ls=ls: cum_ref[r0 + i:r0 + i + 1, ls],
                    vb, st_ref[hd], diag_mask, True)
                st_ref[hd] = st_new
                o_ref[rows, ls] = _hgrn_finish(o, scores, vb, gn_ref[:, ls], sg_ref[rows, ls]).astype(o_ref.dtype)

    @pl.when(exact_needed)
    def _():
        def chunk_body(c, carry):
            r0 = pl.multiple_of(c * c_len, c_len)
            rows = pl.ds(r0, c_len)
            cc_ref[...] = cum_ref[rows, :]
            kc_ref[...] = k_ref[rows, :]
            for hd, ls in enumerate(lanes):
                vb = v_ref[rows, ls].astype(BF16)
                o, st_new, scores = _hgrn_stage1(
                    q_ref[rows, ls], kc_ref[:, ls], cc_ref[:, ls], lambda i, ls=ls: cc_ref[i:i + 1, ls],
                    vb, st_ref[hd], diag_mask, False)
                st_ref[hd] = st_new
                scores = scores + _hgrn_exact_diag(
                    lambda t0, ls=ls: q_ref[pl.ds(r0 + t0, SUBLANES), ls],
                    lambda s, ls=ls: kc_ref[s:s + 1, ls],
                    lambda s, ls=ls: cc_ref[s:s + 1, ls])
                o_ref[rows, ls] = _hgrn_finish(o, scores, vb, gn_ref[:, ls], sg_ref[rows, ls]).astype(o_ref.dtype)
            return carry

        lax.fori_loop(0, n_chunks, chunk_body, 0)


def _hgrn_core(q, k, log_f, v, sg, g_norm, batch, tb=256, heads=4):
    m, d = q.shape
    t = m // batch
    tb = _tile(t, tb)
    heads = _tile(d // HGRN_HEAD, heads)
    wl = heads * HGRN_HEAD
    nt = t // tb
    blk = pl.BlockSpec((tb, wl), lambda b, j, i: (b * nt + i, j))
    vec = pl.BlockSpec((1, wl), lambda b, j, i: (0, j))
    return pl.pallas_call(
        functools.partial(_hgrn_core_kernel, heads=heads),
        grid=(batch, d // wl, nt),
        in_specs=[blk, blk, blk, blk, blk, vec],
        out_specs=blk,
        out_shape=jax.ShapeDtypeStruct((m, d), BF16),
        scratch_shapes=[pltpu.VMEM((heads, HGRN_HEAD, HGRN_HEAD), F32),
                        pltpu.VMEM((tb, wl), F32),
                        pltpu.VMEM((HGRN_CHUNK, wl), F32),
                        pltpu.VMEM((HGRN_CHUNK, wl), F32),
                        pltpu.VMEM((tb, 3 * tb), BF16)],
        compiler_params=_cparams("parallel", "parallel", "arbitrary"), name="hgrn_core",
    )(q, k, log_f, v, sg, g_norm.astype(F32).reshape(1, d))


def kernel(x, norm_gains, s5_a_re, s5_a_im, s5_log_dt, s5_b_re, s5_b_im, s5_c_re, s5_c_im, s5_d,
           s5_w_glu, hgrn_w_in, hgrn_lb_logits, hgrn_g_norm, hgrn_w_out, ffn_w_gate_up, ffn_w_down):
    batch, seq, d = x.shape
    depth = norm_gains.shape[0]
    d_ff = ffn_w_down.shape[1]
    m = batch * seq
    gains = norm_gains.astype(F32)
    lower_bounds = _lower_bounds(hgrn_lb_logits)

    h = x.reshape(m, d).astype(F32)
    a = _norm(h, gains[0, 0], F32)
    f_out = None
    for layer in range(depth):
        j = layer // 2
        if layer % 2 == 0:
            if f_out is not None:
                h, a = _resid_norm(h, f_out, gains[layer - 1, 3], gains[layer, 0], F32)
            s5_ops = _s5_prep(s5_a_re[j], s5_a_im[j], s5_log_dt[j], s5_b_re[j], s5_b_im[j],
                              s5_c_re[j], s5_c_im[j])
            y = _s5_core(a, batch, *s5_ops)
            z = _s5_act(y, a, s5_d[j])
            (mix,) = _matmul(z, s5_w_glu, j, (0, d), d, _ep_glu, (BF16,), bm=2048, name="s5_glu")
        else:
            h, a, q = _matmul_rows(_rows_resid_norm, (h, f_out), (gains[layer - 1, 3], gains[layer, 0]),
                                   (F32, BF16), hgrn_w_in, j, (0,), d, _ep_silu, (F32,), bn=512, name="hgrn_q")
            log_f, key = _matmul(a, hgrn_w_in, j, (d,), d, _ep_forget, (F32, F32), vecs=(lower_bounds[layer],),
                                 bm=2048, m_split=2, name="hgrn_f")
            (val,) = _matmul(a, hgrn_w_in, j, (2 * d,), d, _ep_identity, (F32,), bn=512, name="hgrn_v")
            (sg,) = _matmul(a, hgrn_w_in, j, (3 * d,), d, _ep_silu, (BF16,), bn=512, name="hgrn_g")
            o = _hgrn_core(q, key, log_f, val, sg, hgrn_g_norm[j], batch)
            (mix,) = _matmul(o, hgrn_w_out, j, (0,), d, _ep_identity, (BF16,), bn=512, name="hgrn_out")
        h, a_ffn = _resid_norm(h, mix, gains[layer, 1], gains[layer, 2], BF16)
        (act,) = _matmul(a_ffn, ffn_w_gate_up, layer, (0, d_ff), d_ff, _ep_swiglu, (BF16,), bm=2048, name="ffn_up")
        (f_out,) = _matmul(act, ffn_w_down, layer, (0,), d, _ep_identity, (BF16,), x_buffers=1, name="ffn_down")
    h, _ = _resid_norm(h, f_out, gains[depth - 1, 3], None, None)
    return h.reshape(batch, seq, d).astype(x.dtype)
```

```python
import functools

import jax
import jax.numpy as jnp
from jax import lax
from jax.experimental import pallas as pl
from jax.experimental.pallas import tpu as pltpu

F32 = jnp.float32
BF16 = jnp.bfloat16
HIGHEST = lax.Precision.HIGHEST

RMS_EPS = 1e-6
S5_GROUP = 16
S5_STATE = 64
S5_EIG_CLIP = 1e-4
S5_CHUNK = 16
HGRN_HEAD = 128
HGRN_CHUNK = 64
HGRN_SUB = 16
HGRN_SAFE_DECAY = 60.0
LANES = 128
SUBLANES = 8
VMEM_LIMIT = 56 * 1024 * 1024


def _cparams(*sem):
    return pltpu.CompilerParams(dimension_semantics=sem, vmem_limit_bytes=VMEM_LIMIT)


def _tile(n, want):
    t = min(n, want)
    assert n % t == 0, (n, want)
    return t


def _lower_bounds_kernel(logit_ref, out_ref):
    x = logit_ref[...]
    e = jnp.exp(x - jnp.max(x, axis=0, keepdims=True))
    p = e / jnp.sum(e, axis=0, keepdims=True)
    acc = jnp.zeros_like(p[0:1])
    rows = [acc]
    for layer in range(1, x.shape[0]):
        acc = acc + p[layer:layer + 1]
        rows.append(acc)
    out_ref[...] = jnp.concatenate(rows, axis=0)


def _lower_bounds(logits):
    return pl.pallas_call(
        _lower_bounds_kernel,
        out_shape=jax.ShapeDtypeStruct(logits.shape, F32),
        name="lower_bounds",
    )(logits.astype(F32))


def _rms(x, gain):
    return x * lax.rsqrt(jnp.mean(x * x, axis=-1, keepdims=True) + RMS_EPS) * gain


def _norm_kernel(x_ref, g_ref, o_ref):
    o_ref[...] = _rms(x_ref[...], g_ref[...]).astype(o_ref.dtype)


def _norm(x, gain, out_dtype, bm=256):
    m, d = x.shape
    bm = _tile(m, bm)
    row = pl.BlockSpec((bm, d), lambda i: (i, 0))
    vec = pl.BlockSpec((1, d), lambda i: (0, 0))
    return pl.pallas_call(
        _norm_kernel, grid=(m // bm,), in_specs=[row, vec], out_specs=row,
        out_shape=jax.ShapeDtypeStruct((m, d), out_dtype),
        compiler_params=_cparams("parallel"), name="rms_norm",
    )(x, gain.reshape(1, d))


def _resid_norm_kernel(h_ref, m_ref, gpost_ref, gpre_ref, h_out_ref, a_out_ref):
    h = h_ref[...] + _rms(m_ref[...].astype(F32), gpost_ref[...])
    h_out_ref[...] = h
    a_out_ref[...] = _rms(h, gpre_ref[...]).astype(a_out_ref.dtype)


def _resid_kernel(h_ref, m_ref, gpost_ref, h_out_ref):
    h_out_ref[...] = h_ref[...] + _rms(m_ref[...].astype(F32), gpost_ref[...])


def _resid_norm(h, m_out, g_post, g_pre, a_dtype, bm=256):
    m, d = h.shape
    bm = _tile(m, bm)
    row = pl.BlockSpec((bm, d), lambda i: (i, 0))
    vec = pl.BlockSpec((1, d), lambda i: (0, 0))
    if g_pre is None:
        return pl.pallas_call(
            _resid_kernel, grid=(m // bm,), in_specs=[row, row, vec], out_specs=row,
            out_shape=jax.ShapeDtypeStruct((m, d), F32),
            compiler_params=_cparams("parallel"), name="resid",
        )(h, m_out, g_post.reshape(1, d)), None
    return pl.pallas_call(
        _resid_norm_kernel, grid=(m // bm,), in_specs=[row, row, vec, vec],
        out_specs=[row, row],
        out_shape=[jax.ShapeDtypeStruct((m, d), F32), jax.ShapeDtypeStruct((m, d), a_dtype)],
        compiler_params=_cparams("parallel"), name="resid_norm",
    )(h, m_out, g_post.reshape(1, d), g_pre.reshape(1, d))


def _mm_kernel(*refs, n_w, n_vec, epilogue, m_split):
    x_ref = refs[0]
    w_refs = refs[1:1 + n_w]
    vec_refs = refs[1 + n_w:1 + n_w + n_vec]
    out_refs = refs[1 + n_w + n_vec:]
    ws = [w[...].astype(BF16) for w in w_refs]
    vecs = [v[...] for v in vec_refs]
    part = x_ref.shape[0] // m_split
    for p in range(m_split):
        rows = slice(p * part, (p + 1) * part)
        accs = [jnp.dot(x_ref[rows, :], w, preferred_element_type=F32) for w in ws]
        outs = epilogue(*accs, *vecs)
        for o_ref, val in zip(out_refs, outs):
            o_ref[rows, :] = val.astype(o_ref.dtype)


def _matmul(x, w, layer, col_starts, n_cols, epilogue, out_dtypes, vecs=(), bm=1024, bn=256, x_buffers=2,
            m_split=1, name="matmul"):
    m, kdim = x.shape
    bm = _tile(m, bm)
    bn = _tile(n_cols, bn)
    x_spec = pl.BlockSpec((bm, kdim), lambda i, j: (i, 0), pipeline_mode=pl.Buffered(x_buffers))
    w_specs = []
    for s in col_starts:
        assert s % bn == 0
        w_specs.append(pl.BlockSpec((None, kdim, bn),
                                    functools.partial(lambda i, j, off: (layer, 0, j + off), off=s // bn)))
    vec_spec = pl.BlockSpec((1, bn), lambda i, j: (0, j))
    out_spec = pl.BlockSpec((bm, bn), lambda i, j: (i, j))
    outs = pl.pallas_call(
        functools.partial(_mm_kernel, n_w=len(col_starts), n_vec=len(vecs), epilogue=epilogue, m_split=m_split),
        grid=(m // bm, n_cols // bn),
        in_specs=[x_spec] + w_specs + [vec_spec] * len(vecs),
        out_specs=[out_spec] * len(out_dtypes),
        out_shape=[jax.ShapeDtypeStruct((m, n_cols), dt) for dt in out_dtypes],
        compiler_params=_cparams("parallel", "arbitrary"), name=name,
    )(x, *([w] * len(col_starts)), *[v.reshape(1, n_cols) for v in vecs])
    return outs


def _mm_rows_kernel(*refs, n_rin, n_rvec, n_w, n_vec, n_rout, row_fn, epilogue, n_tiles, n_sub, rblk):
    rin = refs[:n_rin]
    rvec = refs[n_rin:n_rin + n_rvec]
    w_refs = refs[n_rin + n_rvec:n_rin + n_rvec + n_w]
    vec_refs = refs[n_rin + n_rvec + n_w:n_rin + n_rvec + n_w + n_vec]
    outs_at = n_rin + n_rvec + n_w + n_vec
    rout = refs[outs_at:outs_at + n_rout]
    out_refs = refs[outs_at + n_rout:-2]
    x_even_ref, x_odd_ref = refs[-2:]
    i = pl.program_id(0)
    j = pl.program_id(1)

    def step(x_read_ref, x_write_ref):
        vals = row_fn(*[r[...] for r in rin], *[v[...] for v in rvec])
        start = pl.multiple_of(jnp.minimum(j, n_sub - 1) * rblk, rblk)
        x_write_ref[pl.ds(start, rblk), :] = vals[0].astype(x_write_ref.dtype)
        for o_ref, val in zip(rout, vals[1:]):
            o_ref[...] = val.astype(o_ref.dtype)
        if x_read_ref is not None:
            x = x_read_ref[...]
            accs = [jnp.dot(x, w[...].astype(BF16), preferred_element_type=F32) for w in w_refs]
            outs = epilogue(*accs, *[v[...] for v in vec_refs])
            for o_ref, val in zip(out_refs, outs):
                o_ref[...] = val.astype(o_ref.dtype)

    @pl.when(i == 0)
    def _():
        step(None, x_even_ref)

    @pl.when((i > 0) & (i % 2 == 0))
    def _():
        step(x_odd_ref, x_even_ref)

    @pl.when(i % 2 == 1)
    def _():
        step(x_even_ref, x_odd_ref)


def _matmul_rows(row_fn, row_ins, row_vecs, row_out_dtypes, w, layer, col_starts, n_cols, epilogue, out_dtypes,
                 vecs=(), bm=1024, bn=256, name="matmul_rows"):
    m, kdim = row_ins[0].shape
    bm = _tile(m, bm)
    bn = _tile(n_cols, bn)
    n_tiles, n_j = m // bm, n_cols // bn
    n_sub = 1
    while n_sub * 2 <= n_j and bm % (n_sub * 2) == 0 and (bm // (n_sub * 2)) % 16 == 0:
        n_sub *= 2
    rblk = bm // n_sub
    last = n_tiles * n_sub - 1

    def row_idx(i, j):
        return (jnp.minimum(i * n_sub + jnp.minimum(j, n_sub - 1), last), 0)

    def col(i, j):
        return j * jnp.minimum(i, 1)

    row_spec = pl.BlockSpec((rblk, kdim), row_idx)
    rvec_spec = pl.BlockSpec((1, kdim), lambda i, j: (0, 0))
    w_specs = []
    for s in col_starts:
        assert s % bn == 0
        w_specs.append(pl.BlockSpec((None, kdim, bn),
                                    functools.partial(lambda i, j, off: (layer, 0, col(i, j) + off), off=s // bn)))
    vec_spec = pl.BlockSpec((1, bn), lambda i, j: (0, col(i, j)))
    out_spec = pl.BlockSpec((bm, bn), lambda i, j: (jnp.maximum(i - 1, 0), col(i, j)))
    outs = pl.pallas_call(
        functools.partial(_mm_rows_kernel, n_rin=len(row_ins), n_rvec=len(row_vecs), n_w=len(col_starts),
                          n_vec=len(vecs), n_rout=len(row_out_dtypes), row_fn=row_fn, epilogue=epilogue,
                          n_tiles=n_tiles, n_sub=n_sub, rblk=rblk),
        grid=(n_tiles + 1, n_j),
        in_specs=[row_spec] * len(row_ins) + [rvec_spec] * len(row_vecs) + w_specs + [vec_spec] * len(vecs),
        out_specs=[row_spec] * len(row_out_dtypes) + [out_spec] * len(out_dtypes),
        out_shape=([jax.ShapeDtypeStruct((m, kdim), dt) for dt in row_out_dtypes]
                   + [jax.ShapeDtypeStruct((m, n_cols), dt) for dt in out_dtypes]),
        scratch_shapes=[pltpu.VMEM((bm, kdim), BF16), pltpu.VMEM((bm, kdim), BF16)],
        compiler_params=_cparams("arbitrary", "arbitrary"), name=name,
    )(*row_ins, *[v.reshape(1, kdim) for v in row_vecs], *([w] * len(col_starts)),
      *[v.reshape(1, n_cols) for v in vecs])
    return outs


def _rows_resid_norm(h, m_out, g_post, g_pre):
    h_new = h + _rms(m_out.astype(F32), g_post)
    a = _rms(h_new, g_pre)
    return a, h_new, a


def _ep_identity(acc):
    return (acc,)


def _ep_silu(acc):
    return (acc * jax.nn.sigmoid(acc),)


def _ep_glu(val, gate):
    return (val * jax.nn.sigmoid(gate),)


def _ep_swiglu(gate, up):
    return (gate * jax.nn.sigmoid(gate) * up,)


def _ep_forget(f, lb):
    e = jnp.exp(-jnp.abs(f))
    one_e = 1.0 + e
    log_sig = jnp.minimum(f, 0.0) - jnp.log(one_e)
    a = jnp.log(lb)
    b = jnp.log1p(-lb) + log_sig
    log_forget = jnp.maximum(a, b) + jnp.log(1.0 + jnp.exp(-jnp.abs(a - b)))
    key = (1.0 - lb) * jnp.where(f >= 0.0, e, 1.0) / one_e
    return log_forget, key


def _ep_hgrn_fvg(f, v, g, lb):
    return _ep_forget(f, lb) + (v,) + _ep_silu(g)


def _s5_prep_kernel(are_ref, aim_ref, ldt_ref, btr_ref, bti_ref, cre_ref, cim_ref,
                    dt_ref, bac_ref, cpc_ref, al_ref):
    gb = are_ref.shape[0]
    h, big_l = S5_GROUP, S5_CHUNK
    lane = lax.broadcasted_iota(jnp.int32, (h, LANES), 1)
    for g in range(gb):
        rows = slice(g * h, (g + 1) * h)
        lam_re = jnp.minimum(are_ref[g], -S5_EIG_CLIP)
        lam_im = aim_ref[g]
        dt = jnp.exp(ldt_ref[g])
        mag = jnp.exp(lam_re * dt)
        ar = mag * jnp.cos(lam_im * dt)
        ai = mag * jnp.sin(lam_im * dt)
        den = lam_re * lam_re + lam_im * lam_im
        z_re = ((ar - 1.0) * lam_re + ai * lam_im) / den
        z_im = (ai * lam_re - (ar - 1.0) * lam_im) / den
        btr, bti = btr_ref[g], bti_ref[g]
        bbr = z_re * btr - z_im * bti
        bbi = z_re * bti + z_im * btr
        cre, cim = cre_ref[g], cim_ref[g]
        pr = jnp.ones_like(ar)
        pi = jnp.zeros_like(ar)
        cp_rows = []
        for tau in range(big_l + 1):
            cp = jnp.concatenate([cre * pr - cim * pi, -(cre * pi + cim * pr)], axis=-1)
            if tau < big_l:
                cp_rows.append(cp)
                ba = jnp.concatenate([bbr * pr - bbi * pi, bbr * pi + bbi * pr], axis=-1)
                bac_ref[tau, rows, :] = ba.astype(bac_ref.dtype)
            if tau >= 1:
                cpc_ref[tau - 1, rows, :] = cp.astype(cpc_ref.dtype)
            if tau == big_l:
                al_ref[0, g:g + 1, :] = jnp.concatenate([pr, pr], axis=-1)
                al_ref[1, g:g + 1, :] = jnp.concatenate([-pi, pi], axis=-1)
            pr, pi = pr * ar - pi * ai, pr * ai + pi * ar
        cp_all = jnp.concatenate(cp_rows, axis=0)
        bb = jnp.concatenate([bbr, bbi], axis=-1)
        krow = lax.dot_general(bb, cp_all, (((1,), (1,)), ((), ())), precision=HIGHEST,
                               preferred_element_type=F32)
        own = (lane >= g * h) & (lane < (g + 1) * h)
        taus_per_vreg = LANES // h
        for tau in range(big_l):
            src = krow[:, (tau // taus_per_vreg) * LANES:(tau // taus_per_vreg + 1) * LANES]
            shift = ((g - tau % taus_per_vreg) * h) % LANES
            moved = src if shift == 0 else pltpu.roll(src, shift, axis=1)
            dt_ref[tau, rows, :] = jnp.where(own, moved, 0.0).astype(dt_ref.dtype)


def _s5_prep(a_re, a_im, log_dt, b_re, b_im, c_re, c_im):
    g, p = a_re.shape
    h, big_l = S5_GROUP, S5_CHUNK
    gb = LANES // h
    assert 2 * p == LANES and g % gb == 0
    row = lambda z: z.astype(F32).reshape(g, 1, p)
    ldt = jnp.broadcast_to(log_dt.astype(F32)[:, None, None], (g, 1, p))
    bt = lambda z: jnp.swapaxes(z.astype(F32), 1, 2)
    vspec = pl.BlockSpec((gb, 1, p), lambda i: (i, 0, 0))
    mspec = pl.BlockSpec((gb, h, p), lambda i: (i, 0, 0))
    op_spec = pl.BlockSpec((None, big_l, LANES, LANES), lambda i: (i, 0, 0, 0))
    op_shape = jax.ShapeDtypeStruct((g // gb, big_l, LANES, LANES), BF16)
    return pl.pallas_call(
        _s5_prep_kernel, grid=(g // gb,),
        in_specs=[vspec, vspec, vspec, mspec, mspec, mspec, mspec],
        out_specs=[op_spec, op_spec, op_spec, pl.BlockSpec((None, 2, gb, LANES), lambda i: (i, 0, 0, 0))],
        out_shape=[op_shape, op_shape, op_shape, jax.ShapeDtypeStruct((g // gb, 2, gb, LANES), F32)],
        compiler_params=_cparams("parallel"), name="s5_prep",
    )(row(a_re), row(a_im), ldt, bt(b_re), bt(b_im), c_re.astype(F32), c_im.astype(F32))


def _s5_main_kernel(a_ref, dt_ref, bac_ref, cpc_ref, al_ref, y_ref,
                    u2_ref, bt_ref, bdw_ref, bdo_ref, w3_ref, w3s_ref, sp3_ref, *, batch, n_chunks):
    h, big_l = S5_GROUP, S5_CHUNK
    gb = LANES // h
    nrows = batch * n_chunks
    nseq = batch * gb

    @pl.when(pl.program_id(0) == 0)
    def _():
        bt_ref[...] = jnp.zeros_like(bt_ref)
        bdw_ref[...] = jnp.zeros_like(bdw_ref)
        bdo_ref[...] = jnp.zeros_like(bdo_ref)

    for s in range(big_l):
        for t in range(s, big_l):
            bt_ref[s * LANES:(s + 1) * LANES, t * LANES:(t + 1) * LANES] = dt_ref[t - s]
        for g in range(gb):
            rows = slice(s * LANES + g * h, s * LANES + (g + 1) * h)
            cols = slice(g * LANES, (g + 1) * LANES)
            bdw_ref[rows, cols] = bac_ref[big_l - 1 - s, g * h:(g + 1) * h, :]
            bdo_ref[rows, cols] = cpc_ref[s, g * h:(g + 1) * h, :]

    for s in range(big_l):
        u2_ref[:, s * LANES:(s + 1) * LANES] = a_ref[pl.ds(s, nrows, stride=big_l), :].astype(BF16)
    u2 = u2_ref[...]

    win = jnp.dot(u2, bdw_ref[...], preferred_element_type=F32)
    for b in range(batch):
        for g in range(gb):
            w3_ref[pl.ds(b * gb + g, n_chunks, stride=nseq), :] = (
                win[b * n_chunks:(b + 1) * n_chunks, g * LANES:(g + 1) * LANES])
    w3s_ref[...] = pltpu.roll(w3_ref[...], LANES // 2, axis=1)
    a1 = jnp.concatenate([al_ref[0]] * batch, axis=0)
    a2 = jnp.concatenate([al_ref[1]] * batch, axis=0)

    def step(c, carry):
        st, st_sw = carry
        base = pl.multiple_of(c * nseq, nseq)
        sp3_ref[pl.ds(base, nseq), :] = st
        new = a1 * st + a2 * st_sw + w3_ref[pl.ds(base, nseq), :]
        new_sw = a1 * st_sw - a2 * st + w3s_ref[pl.ds(base, nseq), :]
        return new, new_sw

    zero = jnp.zeros((nseq, LANES), F32)
    lax.fori_loop(0, n_chunks, step, (zero, zero), unroll=True)
    sprev =jnp.concatenate(
        [jnp.concatenate([sp3_ref[pl.ds(b * gb + g, n_chunks, stride=nseq), :] for b in range(batch)], axis=0)
         for g in range(gb)], axis=1).astype(BF16)

    nt = (((1,), (1,)), ((), ()))
    pair = 2 * LANES
    for tp in range(big_l * LANES // pair):
        cols = slice(tp * pair, (tp + 1) * pair)
        live = (tp + 1) * pair
        y2 = (jnp.dot(u2_ref[:, 0:live], bt_ref[0:live, cols], preferred_element_type=F32)
              + lax.dot_general(sprev, bdo_ref[cols, :], nt, preferred_element_type=F32))
        y_ref[pl.ds(2 * tp, nrows, stride=big_l), :] = y2[:, 0:LANES]
        y_ref[pl.ds(2 * tp + 1, nrows, stride=big_l), :] = y2[:, LANES:pair]


def _s5_core(a, batch, dt_op, bac, cpc, al):
    m, d = a.shape
    t = m // batch
    big_l = S5_CHUNK
    gb = LANES // S5_GROUP
    n_chunks = t // big_l
    nrows = batch * n_chunks
    width = big_l * LANES
    tok = pl.BlockSpec((m, LANES), lambda i: (0, i))
    op_spec = pl.BlockSpec((None, big_l, LANES, LANES), lambda i: (i, 0, 0, 0))
    return pl.pallas_call(
        functools.partial(_s5_main_kernel, batch=batch, n_chunks=n_chunks),
        grid=(d // LANES,),
        in_specs=[tok, op_spec, op_spec, op_spec, pl.BlockSpec((None, 2, gb, LANES), lambda i: (i, 0, 0, 0))],
        out_specs=tok,
        out_shape=jax.ShapeDtypeStruct((m, d), F32),
        scratch_shapes=[pltpu.VMEM((nrows, width), BF16),
                        pltpu.VMEM((width, width), BF16),
                        pltpu.VMEM((width, gb * LANES), BF16),
                        pltpu.VMEM((width, gb * LANES), BF16),
                        pltpu.VMEM((n_chunks * batch * gb, LANES), F32),
                        pltpu.VMEM((n_chunks * batch * gb, LANES), F32),
                        pltpu.VMEM((n_chunks * batch * gb, LANES), F32)],
        compiler_params=_cparams("arbitrary"), name="s5_main",
    )(a, dt_op, bac, cpc, al)


def _s5_act_kernel(y_ref, a_ref, d_ref, z_ref):
    z_ref[...] = jax.nn.gelu(y_ref[...] + d_ref[...] * a_ref[...]).astype(z_ref.dtype)


def _s5_act(y, a, d_skip, bm=256):
    m, d = y.shape
    bm = _tile(m, bm)
    row = pl.BlockSpec((bm, d), lambda i: (i, 0))
    vec = pl.BlockSpec((1, d), lambda i: (0, 0))
    return pl.pallas_call(
        _s5_act_kernel, grid=(m // bm,), in_specs=[row, row, vec], out_specs=row,
        out_shape=jax.ShapeDtypeStruct((m, d), BF16),
        compiler_params=_cparams("parallel"), name="s5_act",
    )(y, a, d_skip.astype(F32).reshape(1, d))


def _hgrn_stage1(q, k, cum, crow, vb, st, diag_mask, with_diag):
    nt = (((1,), (1,)), ((), ()))
    tn = (((0,), (0,)), ((), ()))
    r15, r31, r47, r63 = crow(15), crow(31), crow(47), crow(63)

    def zrows(n):
        return jnp.zeros((n, HGRN_HEAD), F32)

    q0 = q * jnp.exp(cum)
    q15 = q[16:32] * jnp.exp(cum[16:32] - r15)
    q31 = q[32:64] * jnp.exp(cum[32:64] - r31)
    q47 = q[48:64] * jnp.exp(cum[48:64] - r47)
    k63 = k * jnp.exp(r63 - cum)
    k31 = k[0:32] * jnp.exp(r31 - cum[0:32])
    k15 = k[0:16] * jnp.exp(r15 - cum[0:16])
    k47 = k[32:48] * jnp.exp(r47 - cum[32:48])
    o_state = lax.dot_general(q0.astype(BF16), st.astype(BF16), nt, preferred_element_type=F32)
    st_new = st * jnp.exp(r63) + lax.dot_general(vb, k63.astype(BF16), tn, preferred_element_type=F32)
    q_cat = jnp.concatenate([
        jnp.concatenate([zrows(32), q31], axis=0),
        jnp.concatenate([zrows(16), q15, zrows(32)], axis=0),
        jnp.concatenate([zrows(48), q47], axis=0)], axis=-1).astype(BF16)
    k_cat = jnp.concatenate([
        jnp.concatenate([k31, zrows(32)], axis=0),
        jnp.concatenate([k15, zrows(48)], axis=0),
        jnp.concatenate([zrows(32), k47, zrows(16)], axis=0)], axis=-1).astype(BF16)
    scores = lax.dot_general(q_cat, k_cat, nt, preferred_element_type=F32)
    if with_diag:
        q_d = jnp.concatenate([q0[0:16], q15, q31[0:16], q47], axis=0).astype(BF16)
        k_d = jnp.concatenate([
            k[0:16] * jnp.exp(jnp.minimum(-cum[0:16], HGRN_SAFE_DECAY)),
            k[16:32] * jnp.exp(jnp.minimum(r15 - cum[16:32], HGRN_SAFE_DECAY)),
            k[32:48] * jnp.exp(jnp.minimum(r31 - cum[32:48], HGRN_SAFE_DECAY)),
            k[48:64] * jnp.exp(jnp.minimum(r47 - cum[48:64], HGRN_SAFE_DECAY))], axis=0).astype(BF16)
        scores = scores + jnp.where(diag_mask, lax.dot_general(q_d, k_d, nt, preferred_element_type=F32), 0.0)
    return o_state, st_new, scores


def _hgrn_exact_diag(qtile, krow, crow):
    c_len, sub = HGRN_CHUNK, HGRN_SUB
    row8 = lax.broadcasted_iota(jnp.int32, (SUBLANES, HGRN_HEAD), 0)
    lane8 = lax.broadcasted_iota(jnp.int32, (SUBLANES, HGRN_HEAD), 1)
    pairs = [(t0, list(range((t0 // sub) * sub, t0 + SUBLANES))) for t0 in range(0, c_len, SUBLANES)]
    e_tiles = []
    for t0, s_list in pairs:
        qt = qtile(t0)
        ct = jnp.concatenate([crow(t0 + i) for i in range(SUBLANES)], axis=0)
        for s in s_list:
            e = qt * krow(s) * jnp.exp(ct - crow(s))
            if s >= t0:
                e = jnp.where(row8 >= s - t0, e, 0.0)
            e_tiles.append(e)
    e_all = jnp.concatenate(e_tiles, axis=0).astype(BF16)
    sums = jnp.dot(e_all, jnp.ones((HGRN_HEAD, HGRN_HEAD), BF16), preferred_element_type=F32)
    diag_tiles = []
    idx = 0
    for t0, s_list in pairs:
        acc = jnp.zeros((SUBLANES, HGRN_HEAD), F32)
        for s in s_list:
            acc = jnp.where(lane8 == s, sums[idx * SUBLANES:(idx + 1) * SUBLANES], acc)
            idx += 1
        diag_tiles.append(acc)
    return jnp.concatenate(diag_tiles, axis=0)[:, 0:c_len]


def _hgrn_finish(o, scores, vb, gn, sg):
    o = o + jnp.dot(scores.astype(BF16), vb, preferred_element_type=F32)
    o = o * lax.rsqrt(jnp.mean(o * o, axis=-1, keepdims=True) + RMS_EPS)
    return o * gn * sg.astype(F32)


def _hgrn_core_kernel(q_ref, k_ref, lf_ref, v_ref, sg_ref, gn_ref, o_ref, st_ref, cum_ref, kc_ref, cc_ref,
                      tri_ref, *, heads):
    c_len, sub, hd_dim = HGRN_CHUNK, HGRN_SUB, HGRN_HEAD
    tb = q_ref.shape[0]
    n_chunks = tb // c_len
    assert c_len == 4 * sub

    ri = lax.broadcasted_iota(jnp.int32, (c_len, c_len), 0)
    ci = lax.broadcasted_iota(jnp.int32, (c_len, c_len), 1)
    diag_mask = (ri >= ci) & ((ri // sub) == (ci // sub))
    lanes = [slice(hd * hd_dim, (hd + 1) * hd_dim) for hd in range(heads)]

    @pl.when(pl.program_id(2) == 0)
    def _():
        st_ref[...] = jnp.zeros_like(st_ref)
        rb = lax.broadcasted_iota(jnp.int32, (tb, tb), 0)
        cb = lax.broadcasted_iota(jnp.int32, (tb, tb), 1)
        tri = ((rb >= cb) & ((rb // c_len) == (cb // c_len))).astype(BF16)
        tri_ref[...] = jnp.concatenate([tri, tri, tri], axis=1)

    lf = lf_ref[...]
    lf_hi = lf.astype(BF16)
    rest = lf - lf_hi.astype(F32)
    lf_mid = rest.astype(BF16)
    lf_lo = (rest - lf_mid.astype(F32)).astype(BF16)
    cum_ref[...] = jnp.dot(tri_ref[...], jnp.concatenate([lf_hi, lf_mid, lf_lo], axis=0),
                           preferred_element_type=F32)
    block_decay = -jnp.sum(lf.reshape(tb // sub, sub, lf.shape[1]), axis=1)
    exact_needed = jnp.max(block_decay) > HGRN_SAFE_DECAY

    @pl.when(jnp.logical_not(exact_needed))
    def _():
        for c in range(n_chunks):
            r0 = c * c_len
            rows = slice(r0, r0 + c_len)
            for hd, ls in enumerate(lanes):
                vb = v_ref[rows, ls].astype(BF16)
                o, st_new, scores = _hgrn_stage1(
                    q_ref[rows, ls], k_ref[rows, ls], cum_ref[rows, ls],
                    lambda i, r0=r0, ls=ls: cum_ref[r0 + i:r0 + i + 1, ls],
                    vb, st_ref[hd], diag_mask, True)
                st_ref[hd] = st_new
                o_ref[rows, ls] = _hgrn_finish(o, scores, vb, gn_ref[:, ls], sg_ref[rows, ls]).astype(o_ref.dtype)

    @pl.when(exact_needed)
    def _():
        def chunk_body(c, carry):
            r0 = pl.multiple_of(c * c_len, c_len)
            rows = pl.ds(r0, c_len)
            cc_ref[...] = cum_ref[rows, :]
            kc_ref[...] = k_ref[rows, :]
            for hd, ls in enumerate(lanes):
                vb = v_ref[rows, ls].astype(BF16)
                o, st_new, scores = _hgrn_stage1(
                    q_ref[rows, ls], kc_ref[:, ls], cc_ref[:, ls], lambda i, ls=ls: cc_ref[i:i + 1, ls],
                    vb, st_ref[hd], diag_mask, False)
                st_ref[hd] = st_new
                scores = scores + _hgrn_exact_diag(
                    lambda t0, ls=ls: q_ref[pl.ds(r0 + t0, SUBLANES), ls],
                    lambda s, ls=ls: kc_ref[s:s + 1, ls],
                    lambda s, ls=ls: cc_ref[s:s + 1, ls])
                o_ref[rows, ls] = _hgrn_finish(o, scores, vb, gn_ref[:, ls], sg_ref[rows, ls]).astype(o_ref.dtype)
            return carry

        lax.fori_loop(0, n_chunks, chunk_body, 0)


def _hgrn_core(q, k, log_f, v, sg, g_norm, batch, tb=256, heads=4):
    m, d = q.shape
    t = m // batch
    tb = _tile(t, tb)
    heads = _tile(d // HGRN_HEAD, heads)
    wl = heads * HGRN_HEAD
    nt = t // tb
    blk = pl.BlockSpec((tb, wl), lambda b, j, i: (b * nt + i, j))
    vec = pl.BlockSpec((1, wl), lambda b, j, i: (0, j))
    return pl.pallas_call(
        functools.partial(_hgrn_core_kernel, heads=heads),
        grid=(batch, d // wl, nt),
        in_specs=[blk, blk, blk, blk, blk, vec],
        out_specs=blk,
        out_shape=jax.ShapeDtypeStruct((m, d), BF16),
        scratch_shapes=[pltpu.VMEM((heads, HGRN_HEAD, HGRN_HEAD), F32),
                        pltpu.VMEM((tb, wl), F32),
                        pltpu.VMEM((HGRN_CHUNK, wl), F32),
                        pltpu.VMEM((HGRN_CHUNK, wl), F32),
                        pltpu.VMEM((tb, 3 * tb), BF16)],
        compiler_params=_cparams("parallel", "parallel", "arbitrary"), name="hgrn_core",
    )(q, k, log_f, v, sg, g_norm.astype(F32).reshape(1, d))


def kernel(x, norm_gains, s5_a_re, s5_a_im, s5_log_dt, s5_b_re, s5_b_im, s5_c_re, s5_c_im, s5_d,
           s5_w_glu, hgrn_w_in, hgrn_lb_logits, hgrn_g_norm, hgrn_w_out, ffn_w_gate_up, ffn_w_down):
    batch, seq, d = x.shape
    depth = norm_gains.shape[0]
    d_ff = ffn_w_down.shape[1]
    m = batch * seq
    gains = norm_gains.astype(F32)
    lower_bounds = _lower_bounds(hgrn_lb_logits)

    h = x.reshape(m, d).astype(F32)
    a = _norm(h, gains[0, 0], F32)
    f_out = None
    for layer in range(depth):
        j = layer // 2
        if layer % 2 == 0:
            if f_out is not None:
                h, a = _resid_norm(h, f_out, gains[layer - 1, 3], gains[layer, 0], F32)
            s5_ops = _s5_prep(s5_a_re[j], s5_a_im[j], s5_log_dt[j], s5_b_re[j], s5_b_im[j],
                              s5_c_re[j], s5_c_im[j])
            y = _s5_core(a, batch, *s5_ops)
            z = _s5_act(y, a, s5_d[j])
            (mix,) = _matmul(z, s5_w_glu, j, (0, d), d, _ep_glu, (BF16,), bm=2048, name="s5_glu")
        else:
            h, a, q = _matmul_rows(_rows_resid_norm, (h, f_out), (gains[layer - 1, 3], gains[layer, 0]),
                                   (F32, BF16), hgrn_w_in, j, (0,), d, _ep_silu, (F32,), bn=512, name="hgrn_q")
            log_f, key, val, sg = _matmul(a, hgrn_w_in, j, (d, 2 * d, 3 * d), d, _ep_hgrn_fvg, (F32, F32, F32, BF16),
                                          vecs=(lower_bounds[layer],), m_split=2, name="hgrn_fvg")
            o = _hgrn_core(q, key, log_f, val, sg, hgrn_g_norm[j], batch)
            (mix,) = _matmul(o, hgrn_w_out, j, (0,), d, _ep_identity, (BF16,), bn=512, name="hgrn_out")
        h, a_ffn = _resid_norm(h, mix, gains[layer, 1], gains[layer, 2], BF16)
        (act,) = _matmul(a_ffn, ffn_w_gate_up, layer, (0, d_ff), d_ff, _ep_swiglu, (BF16,), bm=2048, name="ffn_up")
        (f_out,) = _matmul(act, ffn_w_down, layer, (0,), d, _ep_identity, (BF16,), x_buffers=1, name="ffn_down")
    h, _ = _resid_norm(h, f_out, gains[depth - 1, 3], None, None)
    return h.reshape(batch, seq, d).astype(x.dtype)
```

```python
import functools

import jax
import jax.numpy as jnp
from jax import lax
from jax.experimental import pallas as pl
from jax.experimental.pallas import tpu as pltpu

F32 = jnp.float32
BF16 = jnp.bfloat16
HIGHEST = lax.Precision.HIGHEST

RMS_EPS = 1e-6
S5_GROUP = 16
S5_STATE = 64
S5_EIG_CLIP = 1e-4
S5_CHUNK = 16
HGRN_HEAD = 128
HGRN_CHUNK = 64
HGRN_SUB = 16
HGRN_SAFE_DECAY = 60.0
LANES = 128
SUBLANES = 8
VMEM_LIMIT = 56 * 1024 * 1024


def _cparams(*sem):
    return pltpu.CompilerParams(dimension_semantics=sem, vmem_limit_bytes=VMEM_LIMIT)


def _tile(n, want):
    t = min(n, want)
    assert n % t == 0, (n, want)
    return t


def _lower_bounds_kernel(logit_ref, out_ref):
    x = logit_ref[...]
    e = jnp.exp(x - jnp.max(x, axis=0, keepdims=True))
    p = e / jnp.sum(e, axis=0, keepdims=True)
    acc = jnp.zeros_like(p[0:1])
    rows = [acc]
    for layer in range(1, x.shape[0]):
        acc = acc + p[layer:layer + 1]
        rows.append(acc)
    out_ref[...] = jnp.concatenate(rows, axis=0)


def _lower_bounds(logits):
    return pl.pallas_call(
        _lower_bounds_kernel,
        out_shape=jax.ShapeDtypeStruct(logits.shape, F32),
        name="lower_bounds",
    )(logits.astype(F32))


def _rms(x, gain):
    return x * lax.rsqrt(jnp.mean(x * x, axis=-1, keepdims=True) + RMS_EPS) * gain


def _norm_kernel(x_ref, g_ref, o_ref):
    o_ref[...] = _rms(x_ref[...], g_ref[...]).astype(o_ref.dtype)


def _norm(x, gain, out_dtype, bm=256):
    m, d = x.shape
    bm = _tile(m, bm)
    row = pl.BlockSpec((bm, d), lambda i: (i, 0))
    vec = pl.BlockSpec((1, d), lambda i: (0, 0))
    return pl.pallas_call(
        _norm_kernel, grid=(m // bm,), in_specs=[row, vec], out_specs=row,
        out_shape=jax.ShapeDtypeStruct((m, d), out_dtype),
        compiler_params=_cparams("parallel"), name="rms_norm",
    )(x, gain.reshape(1, d))


def _resid_norm_kernel(h_ref, m_ref, gpost_ref, gpre_ref, h_out_ref, a_out_ref):
    h = h_ref[...] + _rms(m_ref[...].astype(F32), gpost_ref[...])
    h_out_ref[...] = h
    a_out_ref[...] = _rms(h, gpre_ref[...]).astype(a_out_ref.dtype)


def _resid_kernel(h_ref, m_ref, gpost_ref, h_out_ref):
    h_out_ref[...] = h_ref[...] + _rms(m_ref[...].astype(F32), gpost_ref[...])


def _resid_norm(h, m_out, g_post, g_pre, a_dtype, bm=256):
    m, d = h.shape
    bm = _tile(m, bm)
    row = pl.BlockSpec((bm, d), lambda i: (i, 0))
    vec = pl.BlockSpec((1, d), lambda i: (0, 0))
    if g_pre is None:
        return pl.pallas_call(
            _resid_kernel, grid=(m // bm,), in_specs=[row, row, vec], out_specs=row,
            out_shape=jax.ShapeDtypeStruct((m, d), F32),
            compiler_params=_cparams("parallel"), name="resid",
        )(h, m_out, g_post.reshape(1, d)), None
    return pl.pallas_call(
        _resid_norm_kernel, grid=(m // bm,), in_specs=[row, row, vec, vec],
        out_specs=[row, row],
        out_shape=[jax.ShapeDtypeStruct((m, d), F32), jax.ShapeDtypeStruct((m, d), a_dtype)],
        compiler_params=_cparams("parallel"), name="resid_norm",
    )(h, m_out, g_post.reshape(1, d), g_pre.reshape(1, d))


def _mm_kernel(*refs, n_w, n_vec, epilogue, m_split):
    x_ref = refs[0]
    w_refs = refs[1:1 + n_w]
    vec_refs = refs[1 + n_w:1 + n_w + n_vec]
    out_refs = refs[1 + n_w + n_vec:]
    ws = [w[...].astype(BF16) for w in w_refs]
    vecs = [v[...] for v in vec_refs]
    part = x_ref.shape[0] // m_split
    for p in range(m_split):
        rows = slice(p * part, (p + 1) * part)
        accs = [jnp.dot(x_ref[rows, :], w, preferred_element_type=F32) for w in ws]
        outs = epilogue(*accs, *vecs)
        for o_ref, val in zip(out_refs, outs):
            o_ref[rows, :] = val.astype(o_ref.dtype)


def _matmul(x, w, layer, col_starts, n_cols, epilogue, out_dtypes, vecs=(), bm=1024, bn=256, x_buffers=2,
            m_split=1, name="matmul"):
    m, kdim = x.shape
    bm = _tile(m, bm)
    bn = _tile(n_cols, bn)
    x_spec = pl.BlockSpec((bm, kdim), lambda i, j: (i, 0), pipeline_mode=pl.Buffered(x_buffers))
    w_specs = []
    for s in col_starts:
        assert s % bn == 0
        w_specs.append(pl.BlockSpec((None, kdim, bn),
                                    functools.partial(lambda i, j, off: (layer, 0, j + off), off=s // bn)))
    vec_spec = pl.BlockSpec((1, bn), lambda i, j: (0, j))
    out_spec = pl.BlockSpec((bm, bn), lambda i, j: (i, j))
    outs = pl.pallas_call(
        functools.partial(_mm_kernel, n_w=len(col_starts), n_vec=len(vecs), epilogue=epilogue, m_split=m_split),
        grid=(m // bm, n_cols // bn),
        in_specs=[x_spec] + w_specs + [vec_spec] * len(vecs),
        out_specs=[out_spec] * len(out_dtypes),
        out_shape=[jax.ShapeDtypeStruct((m, n_cols), dt) for dt in out_dtypes],
        compiler_params=_cparams("parallel", "arbitrary"), name=name,
    )(x, *([w] * len(col_starts)), *[v.reshape(1, n_cols) for v in vecs])
    return outs


def _mm_rows_kernel(*refs, n_rin, n_rvec, n_w, n_vec, n_rout, row_fn, epilogue, n_tiles, n_sub, rblk):
    rin = refs[:n_rin]
    rvec = refs[n_rin:n_rin + n_rvec]
    w_refs = refs[n_rin + n_rvec:n_rin + n_rvec + n_w]
    vec_refs = refs[n_rin + n_rvec + n_w:n_rin + n_rvec + n_w + n_vec]
    outs_at = n_rin + n_rvec + n_w + n_vec
    rout = refs[outs_at:outs_at + n_rout]
    out_refs = refs[outs_at + n_rout:-2]
    x_even_ref, x_odd_ref = refs[-2:]
    i = pl.program_id(0)
    j = pl.program_id(1)

    def step(x_read_ref, x_write_ref):
        vals = row_fn(*[r[...] for r in rin], *[v[...] for v in rvec])
        start = pl.multiple_of(jnp.minimum(j, n_sub - 1) * rblk, rblk)
        x_write_ref[pl.ds(start, rblk), :] = vals[0].astype(x_write_ref.dtype)
        for o_ref, val in zip(rout, vals[1:]):
            o_ref[...] = val.astype(o_ref.dtype)
        if x_read_ref is not None:
            x = x_read_ref[...]
            accs = [jnp.dot(x, w[...].astype(BF16), preferred_element_type=F32) for w in w_refs]
            outs = epilogue(*accs, *[v[...] for v in vec_refs])
            for o_ref, val in zip(out_refs, outs):
                o_ref[...] = val.astype(o_ref.dtype)

    @pl.when(i == 0)
    def _():
        step(None, x_even_ref)

    @pl.when((i > 0) & (i % 2 == 0))
    def _():
        step(x_odd_ref, x_even_ref)

    @pl.when(i % 2 == 1)
    def _():
        step(x_even_ref, x_odd_ref)


def _matmul_rows(row_fn, row_ins, row_vecs, row_out_dtypes, w, layer, col_starts, n_cols, epilogue, out_dtypes,
                 vecs=(), bm=1024, bn=256, name="matmul_rows"):
    m, kdim = row_ins[0].shape
    bm = _tile(m, bm)
    bn = _tile(n_cols, bn)
    n_tiles, n_j = m // bm, n_cols // bn
    n_sub = 1
    while n_sub * 2 <= n_j and bm % (n_sub * 2) == 0 and (bm // (n_sub * 2)) % 16 == 0:
        n_sub *= 2
    rblk = bm // n_sub
    last = n_tiles * n_sub - 1

    def row_idx(i, j):
        return (jnp.minimum(i * n_sub + jnp.minimum(j, n_sub - 1), last), 0)

    def col(i, j):
        return j * jnp.minimum(i, 1)

    row_spec = pl.BlockSpec((rblk, kdim), row_idx)
    rvec_spec = pl.BlockSpec((1, kdim), lambda i, j: (0, 0))
    w_specs = []
    for s in col_starts:
        assert s % bn == 0
        w_specs.append(pl.BlockSpec((None, kdim, bn),
                                    functools.partial(lambda i, j, off: (layer, 0, col(i, j) + off), off=s // bn)))
    vec_spec = pl.BlockSpec((1, bn), lambda i, j: (0, col(i, j)))
    out_spec = pl.BlockSpec((bm, bn), lambda i, j: (jnp.maximum(i - 1, 0), col(i, j)))
    outs = pl.pallas_call(
        functools.partial(_mm_rows_kernel, n_rin=len(row_ins), n_rvec=len(row_vecs), n_w=len(col_starts),
                          n_vec=len(vecs), n_rout=len(row_out_dtypes), row_fn=row_fn, epilogue=epilogue,
                          n_tiles=n_tiles, n_sub=n_sub, rblk=rblk),
        grid=(n_tiles + 1, n_j),
        in_specs=[row_spec] * len(row_ins) + [rvec_spec] * len(row_vecs) + w_specs + [vec_spec] * len(vecs),
        out_specs=[row_spec] * len(row_out_dtypes) + [out_spec] * len(out_dtypes),
        out_shape=([jax.ShapeDtypeStruct((m, kdim), dt) for dt in row_out_dtypes]
                   + [jax.ShapeDtypeStruct((m, n_cols), dt) for dt in out_dtypes]),
        scratch_shapes=[pltpu.VMEM((bm, kdim), BF16), pltpu.VMEM((bm, kdim), BF16)],
        compiler_params=_cparams("arbitrary", "arbitrary"), name=name,
    )(*row_ins, *[v.reshape(1, kdim) for v in row_vecs], *([w] * len(col_starts)),
      *[v.reshape(1, n_cols) for v in vecs])
    return outs


def _rows_resid_norm(h, m_out, g_post, g_pre):
    h_new = h + _rms(m_out.astype(F32), g_post)
    a = _rms(h_new, g_pre)
    return a, h_new, a


def _ep_identity(acc):
    return (acc,)


def _ep_silu(acc):
    return (acc * jax.nn.sigmoid(acc),)


def _ep_glu(val, gate):
    return (val * jax.nn.sigmoid(gate),)


def _ep_swiglu(gate, up):
    return (gate * jax.nn.sigmoid(gate) * up,)


def _ep_forget(f, lb):
    e = jnp.exp(-jnp.abs(f))
    one_e = 1.0 + e
    log_sig = jnp.minimum(f, 0.0) - jnp.log(one_e)
    a = jnp.log(lb)
    b = jnp.log1p(-lb) + log_sig
    log_forget = jnp.maximum(a, b) + jnp.log(1.0 + jnp.exp(-jnp.abs(a - b)))
    key = (1.0 - lb) * jnp.where(f >= 0.0, e, 1.0) / one_e
    return log_forget, key


def _ep_hgrn_fvg(f, v, g, lb):
    return _ep_forget(f, lb) + (v,) + _ep_silu(g)


def _s5_prep_kernel(are_ref, aim_ref, ldt_ref, btr_ref, bti_ref, cre_ref, cim_ref,
                    dt_ref, bac_ref, cpc_ref, al_ref):
    gb = are_ref.shape[0]
    h, big_l = S5_GROUP, S5_CHUNK
    lane = lax.broadcasted_iota(jnp.int32, (h, LANES), 1)
    for g in range(gb):
        rows = slice(g * h, (g + 1) * h)
        lam_re = jnp.minimum(are_ref[g], -S5_EIG_CLIP)
        lam_im = aim_ref[g]
        dt = jnp.exp(ldt_ref[g])
        mag = jnp.exp(lam_re * dt)
        ar = mag * jnp.cos(lam_im * dt)
        ai = mag * jnp.sin(lam_im * dt)
        den = lam_re * lam_re + lam_im * lam_im
        z_re = ((ar - 1.0) * lam_re + ai * lam_im) / den
        z_im = (ai * lam_re - (ar - 1.0) * lam_im) / den
        btr, bti = btr_ref[g], bti_ref[g]
        bbr = z_re * btr - z_im * bti
        bbi = z_re * bti + z_im * btr
        cre, cim = cre_ref[g], cim_ref[g]
        pr = jnp.ones_like(ar)
        pi = jnp.zeros_like(ar)
        cp_rows = []
        for tau in range(big_l + 1):
            cp = jnp.concatenate([cre * pr - cim * pi, -(cre * pi + cim * pr)], axis=-1)
            if tau < big_l:
                cp_rows.append(cp)
                ba = jnp.concatenate([bbr * pr - bbi * pi, bbr * pi + bbi * pr], axis=-1)
                bac_ref[tau, rows, :] = ba.astype(bac_ref.dtype)
            if tau >= 1:
                cpc_ref[tau - 1, rows, :] = cp.astype(cpc_ref.dtype)
            if tau == big_l:
                al_ref[0, g:g + 1, :] = jnp.concatenate([pr, pr], axis=-1)
                al_ref[1, g:g + 1, :] = jnp.concatenate([-pi, pi], axis=-1)
            pr, pi = pr * ar - pi * ai, pr * ai + pi * ar
        cp_all = jnp.concatenate(cp_rows, axis=0)
        bb = jnp.concatenate([bbr, bbi], axis=-1)
        krow = lax.dot_general(bb, cp_all, (((1,), (1,)), ((), ())), precision=HIGHEST,
                               preferred_element_type=F32)
        own = (lane >= g * h) & (lane < (g + 1) * h)
        taus_per_vreg = LANES // h
        for tau in range(big_l):
            src = krow[:, (tau // taus_per_vreg) * LANES:(tau // taus_per_vreg + 1) * LANES]
            shift = ((g - tau % taus_per_vreg) * h) % LANES
            moved = src if shift == 0 else pltpu.roll(src, shift, axis=1)
            dt_ref[tau, rows, :] = jnp.where(own, moved, 0.0).astype(dt_ref.dtype)


def _s5_prep(a_re, a_im, log_dt, b_re, b_im, c_re, c_im):
    g, p = a_re.shape
    h, big_l = S5_GROUP, S5_CHUNK
    gb = LANES // h
    assert 2 * p == LANES and g % gb == 0
    row = lambda z: z.astype(F32).reshape(g, 1, p)
    ldt = jnp.broadcast_to(log_dt.astype(F32)[:, None, None], (g, 1, p))
    bt = lambda z: jnp.swapaxes(z.astype(F32), 1, 2)
    vspec = pl.BlockSpec((gb, 1, p), lambda i: (i, 0, 0))
    mspec = pl.BlockSpec((gb, h, p), lambda i: (i, 0, 0))
    op_spec = pl.BlockSpec((None, big_l, LANES, LANES), lambda i: (i, 0, 0, 0))
    op_shape = jax.ShapeDtypeStruct((g // gb, big_l, LANES, LANES), BF16)
    return pl.pallas_call(
        _s5_prep_kernel, grid=(g // gb,),
        in_specs=[vspec, vspec, vspec, mspec, mspec, mspec, mspec],
        out_specs=[op_spec, op_spec, op_spec, pl.BlockSpec((None, 2, gb, LANES), lambda i: (i, 0, 0, 0))],
        out_shape=[op_shape, op_shape, op_shape, jax.ShapeDtypeStruct((g // gb, 2, gb, LANES), F32)],
        compiler_params=_cparams("parallel"), name="s5_prep",
    )(row(a_re), row(a_im), ldt, bt(b_re), bt(b_im), c_re.astype(F32), c_im.astype(F32))


def _s5_main_kernel(a_ref, dt_ref, bac_ref, cpc_ref, al_ref, y_ref,
                    u2_ref, bt_ref, bdw_ref, bdo_ref, w3_ref, w3s_ref, sp3_ref, *, batch, n_chunks):
    h, big_l = S5_GROUP, S5_CHUNK
    gb = LANES // h
    nrows = batch * n_chunks
    nseq = batch * gb

    @pl.when(pl.program_id(0) == 0)
    def _():
        bt_ref[...] = jnp.zeros_like(bt_ref)
        bdw_ref[...] = jnp.zeros_like(bdw_ref)
        bdo_ref[...] = jnp.zeros_like(bdo_ref)

    for s in range(big_l):
        for t in range(s, big_l):
            bt_ref[s * LANES:(s + 1) * LANES, t * LANES:(t + 1) * LANES] = dt_ref[t - s]
        for g in range(gb):
            rows = slice(s * LANES + g * h, s * LANES + (g + 1) * h)
            cols = slice(g * LANES, (g + 1) * LANES)
            bdw_ref[rows, cols] = bac_ref[big_l - 1 - s, g * h:(g + 1) * h, :]
            bdo_ref[rows, cols] = cpc_ref[s, g * h:(g + 1) * h, :]

    for s in range(big_l):
        u2_ref[:, s * LANES:(s + 1) * LANES] = a_ref[pl.ds(s, nrows, stride=big_l), :].astype(BF16)
    u2 = u2_ref[...]

    win = jnp.dot(u2, bdw_ref[...], preferred_element_type=F32)
    for b in range(batch):
        for g in range(gb):
            w3_ref[pl.ds(b * gb + g, n_chunks, stride=nseq), :] = (
                win[b * n_chunks:(b + 1) * n_chunks, g * LANES:(g + 1) * LANES])
    w3s_ref[...] = pltpu.roll(w3_ref[...], LANES // 2, axis=1)
    a1 = jnp.concatenate([al_ref[0]] * batch, axis=0)
    a2 = jnp.concatenate([al_ref[1]] * batch, axis=0)

    def step(c, carry):
        st, st_sw = carry
        base = pl.multiple_of(c * nseq, nseq)
        sp3_ref[pl.ds(base, nseq), :] = st
        new = a1 * st + a2 * st_sw + w3_ref[pl.ds(base, nseq), :]
        new_sw = a1 * st_sw - a2 * st + w3s_ref[pl.ds(base, nseq), :]
        return new, new_sw

    zero = jnp.zeros((nseq, LANES), F32)
    lax.fori_loop(0, n_chunks, step, (zero, zero), unroll=True)
    sprev =jnp.concatenate(
        [jnp.concatenate([sp3_ref[pl.ds(b * gb + g, n_chunks, stride=nseq), :] for b in range(batch)], axis=0)
         for g in range(gb)], axis=1).astype(BF16)

    nt = (((1,), (1,)), ((), ()))
    pair = 2 * LANES
    for tp in range(big_l * LANES // pair):
        cols = slice(tp * pair, (tp + 1) * pair)
        live = (tp + 1) * pair
        y2 = (jnp.dot(u2_ref[:, 0:live], bt_ref[0:live, cols], preferred_element_type=F32)
              + lax.dot_general(sprev, bdo_ref[cols, :], nt, preferred_element_type=F32))
        y_ref[pl.ds(2 * tp, nrows, stride=big_l), :] = y2[:, 0:LANES]
        y_ref[pl.ds(2 * tp + 1, nrows, stride=big_l), :] = y2[:, LANES:pair]


def _s5_core(a, batch, dt_op, bac, cpc, al):
    m, d = a.shape
    t = m // batch
    big_l = S5_CHUNK
    gb = LANES // S5_GROUP
    n_chunks = t // big_l
    nrows = batch * n_chunks
    width = big_l * LANES
    tok = pl.BlockSpec((m, LANES), lambda i: (0, i))
    op_spec = pl.BlockSpec((None, big_l, LANES, LANES), lambda i: (i, 0, 0, 0))
    return pl.pallas_call(
        functools.partial(_s5_main_kernel, batch=batch, n_chunks=n_chunks),
        grid=(d // LANES,),
        in_specs=[tok, op_spec, op_spec, op_spec, pl.BlockSpec((None, 2, gb, LANES), lambda i: (i, 0, 0, 0))],
        out_specs=tok,
        out_shape=jax.ShapeDtypeStruct((m, d), F32),
        scratch_shapes=[pltpu.VMEM((nrows, width), BF16),
                        pltpu.VMEM((width, width), BF16),
                        pltpu.VMEM((width, gb * LANES), BF16),
                        pltpu.VMEM((width, gb * LANES), BF16),
                        pltpu.VMEM((n_chunks * batch * gb, LANES), F32),
                        pltpu.VMEM((n_chunks * batch * gb, LANES), F32),
                        pltpu.VMEM((n_chunks * batch * gb, LANES), F32)],
        compiler_params=_cparams("arbitrary"), name="s5_main",
    )(a, dt_op, bac, cpc, al)


def _s5_act_kernel(y_ref, a_ref, d_ref, z_ref):
    z_ref[...] = jax.nn.gelu(y_ref[...] + d_ref[...] * a_ref[...]).astype(z_ref.dtype)


def _s5_act(y, a, d_skip, bm=256):
    m, d = y.shape
    bm = _tile(m, bm)
    row = pl.BlockSpec((bm, d), lambda i: (i, 0))
    vec = pl.BlockSpec((1, d), lambda i: (0, 0))
    return pl.pallas_call(
        _s5_act_kernel, grid=(m // bm,), in_specs=[row, row, vec], out_specs=row,
        out_shape=jax.ShapeDtypeStruct((m, d), BF16),
        compiler_params=_cparams("parallel"), name="s5_act",
    )(y, a, d_skip.astype(F32).reshape(1, d))


def _hgrn_stage1(q, k, cum, crow, vb, st, diag_mask, with_diag):
    nt = (((1,), (1,)), ((), ()))
    tn = (((0,), (0,)), ((), ()))
    r15, r31, r47, r63 = crow(15), crow(31), crow(47), crow(63)

    def zrows(n):
        return jnp.zeros((n, HGRN_HEAD), F32)

    q0 = q * jnp.exp(cum)
    q15 = q[16:32] * jnp.exp(cum[16:32] - r15)
    q31 = q[32:64] * jnp.exp(cum[32:64] - r31)
    q47 = q[48:64] * jnp.exp(cum[48:64] - r47)
    k63 = k * jnp.exp(r63 - cum)
    k31 = k[0:32] * jnp.exp(r31 - cum[0:32])
    k15 = k[0:16] * jnp.exp(r15 - cum[0:16])
    k47 = k[32:48] * jnp.exp(r47 - cum[32:48])
    o_state = lax.dot_general(q0.astype(BF16), st.astype(BF16), nt, preferred_element_type=F32)
    st_new = st * jnp.exp(r63) + lax.dot_general(vb, k63.astype(BF16), tn, preferred_element_type=F32)
    q_cat = jnp.concatenate([
        jnp.concatenate([zrows(32), q31], axis=0),
        jnp.concatenate([zrows(16), q15, zrows(32)], axis=0),
        jnp.concatenate([zrows(48), q47], axis=0)], axis=-1).astype(BF16)
    k_cat = jnp.concatenate([
        jnp.concatenate([k31, zrows(32)], axis=0),
        jnp.concatenate([k15, zrows(48)], axis=0),
        jnp.concatenate([zrows(32), k47, zrows(16)], axis=0)], axis=-1).astype(BF16)
    scores = lax.dot_general(q_cat, k_cat, nt, preferred_element_type=F32)
    if with_diag:
        q_d = jnp.concatenate([q0[0:16], q15, q31[0:16], q47], axis=0).astype(BF16)
        k_d = jnp.concatenate([
            k[0:16] * jnp.exp(jnp.minimum(-cum[0:16], HGRN_SAFE_DECAY)),
            k[16:32] * jnp.exp(jnp.minimum(r15 - cum[16:32], HGRN_SAFE_DECAY)),
            k[32:48] * jnp.exp(jnp.minimum(r31 - cum[32:48], HGRN_SAFE_DECAY)),
            k[48:64] * jnp.exp(jnp.minimum(r47 - cum[48:64], HGRN_SAFE_DECAY))], axis=0).astype(BF16)
        scores = scores + jnp.where(diag_mask, lax.dot_general(q_d, k_d, nt, preferred_element_type=F32), 0.0)
    return o_state, st_new, scores


def _hgrn_exact_diag(qtile, krow, crow):
    c_len, sub = HGRN_CHUNK, HGRN_SUB
    row8 = lax.broadcasted_iota(jnp.int32, (SUBLANES, HGRN_HEAD), 0)
    lane8 = lax.broadcasted_iota(jnp.int32, (SUBLANES, HGRN_HEAD), 1)
    pairs = [(t0, list(range((t0 // sub) * sub, t0 + SUBLANES))) for t0 in range(0, c_len, SUBLANES)]
    e_tiles = []
    for t0, s_list in pairs:
        qt = qtile(t0)
        ct = jnp.concatenate([crow(t0 + i) for i in range(SUBLANES)], axis=0)
        for s in s_list:
            e = qt * krow(s) * jnp.exp(ct - crow(s))
            if s >= t0:
                e = jnp.where(row8 >= s - t0, e, 0.0)
            e_tiles.append(e)
    e_all = jnp.concatenate(e_tiles, axis=0).astype(BF16)
    sums = jnp.dot(e_all, jnp.ones((HGRN_HEAD, HGRN_HEAD), BF16), preferred_element_type=F32)
    diag_tiles = []
    idx = 0
    for t0, s_list in pairs:
        acc = jnp.zeros((SUBLANES, HGRN_HEAD), F32)
        for s in s_list:
            acc = jnp.where(lane8 == s, sums[idx * SUBLANES:(idx + 1) * SUBLANES], acc)
            idx += 1
        diag_tiles.append(acc)
    return jnp.concatenate(diag_tiles, axis=0)[:, 0:c_len]


def _hgrn_finish(o, scores, vb, gn, sg):
    o = o + jnp.dot(scores.astype(BF16), vb, preferred_element_type=F32)
    o = o * lax.rsqrt(jnp.mean(o * o, axis=-1, keepdims=True) + RMS_EPS)
    return o * gn * sg.astype(F32)


def _hgrn_core_kernel(q_ref, k_ref, lf_ref, v_ref, sg_ref, gn_ref, o_ref, st_ref, cum_ref, kc_ref, cc_ref,
                      tri_ref, *, heads):
    c_len, sub, hd_dim = HGRN_CHUNK, HGRN_SUB, HGRN_HEAD
    tb = q_ref.shape[0]
    n_chunks = tb // c_len
    assert c_len == 4 * sub

    ri = lax.broadcasted_iota(jnp.int32, (c_len, c_len), 0)
    ci = lax.broadcasted_iota(jnp.int32, (c_len, c_len), 1)
    diag_mask = (ri >= ci) & ((ri // sub) == (ci // sub))
    lanes = [slice(hd * hd_dim, (hd + 1) * hd_dim) for hd in range(heads)]

    @pl.when(pl.program_id(2) == 0)
    def _():
        st_ref[...] = jnp.zeros_like(st_ref)
        rb = lax.broadcasted_iota(jnp.int32, (tb, tb), 0)
        cb = lax.broadcasted_iota(jnp.int32, (tb, tb), 1)
        tri = ((rb >= cb) & ((rb // c_len) == (cb // c_len))).astype(BF16)
        tri_ref[...] = jnp.concatenate([tri, tri, tri], axis=1)

    lf = lf_ref[...]
    lf_hi = lf.astype(BF16)
    rest = lf - lf_hi.astype(F32)
    lf_mid = rest.astype(BF16)
    lf_lo = (rest - lf_mid.astype(F32)).astype(BF16)
    cum_ref[...] = jnp.dot(tri_ref[...], jnp.concatenate([lf_hi, lf_mid, lf_lo], axis=0),
                           preferred_element_type=F32)
    block_decay = -jnp.sum(lf.reshape(tb // sub, sub, lf.shape[1]), axis=1)
    exact_needed = jnp.max(block_decay) > HGRN_SAFE_DECAY

    @pl.when(jnp.logical_not(exact_needed))
    def _():
        for c in range(n_chunks):
            r0 = c * c_len
            rows = slice(r0, r0 + c_len)
            for hd, ls in enumerate(lanes):
                vb = v_ref[rows, ls].astype(BF16)
                o, st_new, scores = _hgrn_stage1(
                    q_ref[rows, ls], k_ref[rows, ls], cum_ref[rows, ls],
                    lambda i, r0=r0, ls=ls: cum_ref[r0 + i:r0 + i + 1, ls],
                    vb, st_ref[hd], diag_mask, True)
                st_ref[hd] = st_new
                o_ref[rows, ls] = _hgrn_finish(o, scores, vb, gn_ref[:, ls], sg_ref[rows, ls]).astype(o_ref.dtype)

    @pl.when(exact_needed)
    def _():
        def chunk_body(c, carry):
            r0 = pl.multiple_of(c * c_len, c_len)
            rows = pl.ds(r0, c_len)
            cc_ref[...] = cum_ref[rows, :]
            kc_ref[...] = k_ref[rows, :]
            for hd, ls in enumerate(lanes):
                vb = v_ref[rows, ls].astype(BF16)
                o, st_new, scores = _hgrn_stage1(
                    q_ref[rows, ls], kc_ref[:, ls], cc_ref[:, ls], lambda i, ls=ls: cc_ref[i:i + 1, ls],
                    vb, st_ref[hd], diag_mask, False)
                st_ref[hd] = st_new
                scores = scores + _hgrn_exact_diag(
                    lambda t0, ls=ls: q_ref[pl.ds(r0 + t0, SUBLANES), ls],
                    lambda s, ls=ls: kc_ref[s:s + 1, ls],
                    lambda s, ls=ls: cc_ref[s:s + 1, ls])
                o_ref[rows, ls] = _hgrn_finish(o, scores, vb, gn_ref[:, ls], sg_ref[rows, ls]).astype(o_ref.dtype)
            return carry

        lax.fori_loop(0, n_chunks, chunk_body, 0)


def _hgrn_core(q, k, log_f, v, sg, g_norm, batch, tb=256, heads=8):
    m, d = q.shape
    t = m // batch
    tb = _tile(t, tb)
    heads = _tile(d // HGRN_HEAD, heads)
    wl = heads * HGRN_HEAD
    nt = t // tb
    blk = pl.BlockSpec((tb, wl), lambda b, j, i: (b * nt + i, j))
    vec = pl.BlockSpec((1, wl), lambda b, j, i: (0, j))
    return pl.pallas_call(
        functools.partial(_hgrn_core_kernel, heads=heads),
        grid=(batch, d // wl, nt),
        in_specs=[blk, blk, blk, blk, blk, vec],
        out_specs=blk,
        out_shape=jax.ShapeDtypeStruct((m, d), BF16),
        scratch_shapes=[pltpu.VMEM((heads, HGRN_HEAD, HGRN_HEAD), F32),
                        pltpu.VMEM((tb, wl), F32),
                        pltpu.VMEM((HGRN_CHUNK, wl), F32),
                        pltpu.VMEM((HGRN_CHUNK, wl), F32),
                        pltpu.VMEM((tb, 3 * tb), BF16)],
        compiler_params=_cparams("parallel", "parallel", "arbitrary"), name="hgrn_core",
    )(q, k, log_f, v, sg, g_norm.astype(F32).reshape(1, d))


def kernel(x, norm_gains, s5_a_re, s5_a_im, s5_log_dt, s5_b_re, s5_b_im, s5_c_re, s5_c_im, s5_d,
           s5_w_glu, hgrn_w_in, hgrn_lb_logits, hgrn_g_norm, hgrn_w_out, ffn_w_gate_up, ffn_w_down):
    batch, seq, d = x.shape
    depth = norm_gains.shape[0]
    d_ff = ffn_w_down.shape[1]
    m = batch * seq
    gains = norm_gains.astype(F32)
    lower_bounds = _lower_bounds(hgrn_lb_logits)

    h = x.reshape(m, d).astype(F32)
    a = _norm(h, gains[0, 0], F32)
    f_out = None
    for layer in range(depth):
        j = layer // 2
        if layer % 2 == 0:
            if f_out is not None:
                h, a = _resid_norm(h, f_out, gains[layer - 1, 3], gains[layer, 0], F32)
            s5_ops = _s5_prep(s5_a_re[j], s5_a_im[j], s5_log_dt[j], s5_b_re[j], s5_b_im[j],
                              s5_c_re[j], s5_c_im[j])
            y = _s5_core(a, batch, *s5_ops)
            z = _s5_act(y, a, s5_d[j])
            (mix,) = _matmul(z, s5_w_glu, j, (0, d), d, _ep_glu, (BF16,), bm=2048, name="s5_glu")
        else:
            h, a, q = _matmul_rows(_rows_resid_norm, (h, f_out), (gains[layer - 1, 3], gains[layer, 0]),
                                   (F32, BF16), hgrn_w_in, j, (0,), d, _ep_silu, (F32,), bn=512, name="hgrn_q")
            log_f, key, val, sg = _matmul(a, hgrn_w_in, j, (d, 2 * d, 3 * d), d, _ep_hgrn_fvg, (F32, F32, F32, BF16),
                                          vecs=(lower_bounds[layer],), m_split=2, name="hgrn_fvg")
            o = _hgrn_core(q, key, log_f, val, sg, hgrn_g_norm[j], batch)
            (mix,) = _matmul(o, hgrn_w_out, j, (0,), d, _ep_identity, (BF16,), bn=512, name="hgrn_out")
        h, a_ffn = _resid_norm(h, mix, gains[layer, 1], gains[layer, 2], BF16)
        (act,) = _matmul(a_ffn, ffn_w_gate_up, layer, (0, d_ff), d_ff, _ep_swiglu, (BF16,), bm=2048, name="ffn_up")
        (f_out,) = _matmul(act, ffn_w_down, layer, (0,), d, _ep_identity, (BF16,), x_buffers=1, name="ffn_down")
    h, _ = _resid_norm(h, f_out, gains[depth - 1, 3], None, None)
    return h.reshape(batch, seq, d).astype(x.dtype)
```

```python
import functools

import jax
import jax.numpy as jnp
from jax import lax
from jax.experimental import pallas as pl
from jax.experimental.pallas import tpu as pltpu

F32 = jnp.float32
BF16 = jnp.bfloat16
HIGHEST = lax.Precision.HIGHEST

RMS_EPS = 1e-6
S5_GROUP = 16
S5_STATE = 64
S5_EIG_CLIP = 1e-4
S5_CHUNK = 16
HGRN_HEAD = 128
HGRN_CHUNK = 64
HGRN_SUB = 16
HGRN_SAFE_DECAY = 60.0
LANES = 128
SUBLANES = 8
VMEM_LIMIT = 56 * 1024 * 1024


def _cparams(*sem):
    return pltpu.CompilerParams(dimension_semantics=sem, vmem_limit_bytes=VMEM_LIMIT)


def _tile(n, want):
    t = min(n, want)
    assert n % t == 0, (n, want)
    return t


def _lower_bounds_kernel(logit_ref, out_ref):
    x = logit_ref[...]
    e = jnp.exp(x - jnp.max(x, axis=0, keepdims=True))
    p = e / jnp.sum(e, axis=0, keepdims=True)
    acc = jnp.zeros_like(p[0:1])
    rows = [acc]
    for layer in range(1, x.shape[0]):
        acc = acc + p[layer:layer + 1]
        rows.append(acc)
    out_ref[...] = jnp.concatenate(rows, axis=0)


def _lower_bounds(logits):
    return pl.pallas_call(
        _lower_bounds_kernel,
        out_shape=jax.ShapeDtypeStruct(logits.shape, F32),
        name="lower_bounds",
    )(logits.astype(F32))


def _rms(x, gain):
    return x * lax.rsqrt(jnp.mean(x * x, axis=-1, keepdims=True) + RMS_EPS) * gain


def _norm_kernel(x_ref, g_ref, o_ref):
    o_ref[...] = _rms(x_ref[...], g_ref[...]).astype(o_ref.dtype)


def _norm(x, gain, out_dtype, bm=256):
    m, d = x.shape
    bm = _tile(m, bm)
    row = pl.BlockSpec((bm, d), lambda i: (i, 0))
    vec = pl.BlockSpec((1, d), lambda i: (0, 0))
    return pl.pallas_call(
        _norm_kernel, grid=(m // bm,), in_specs=[row, vec], out_specs=row,
        out_shape=jax.ShapeDtypeStruct((m, d), out_dtype),
        compiler_params=_cparams("parallel"), name="rms_norm",
    )(x, gain.reshape(1, d))


def _resid_norm_kernel(h_ref, m_ref, gpost_ref, gpre_ref, h_out_ref, a_out_ref):
    h = h_ref[...] + _rms(m_ref[...].astype(F32), gpost_ref[...])
    h_out_ref[...] = h
    a_out_ref[...] = _rms(h, gpre_ref[...]).astype(a_out_ref.dtype)


def _resid_kernel(h_ref, m_ref, gpost_ref, h_out_ref):
    h_out_ref[...] = h_ref[...] + _rms(m_ref[...].astype(F32), gpost_ref[...])


def _resid_norm(h, m_out, g_post, g_pre, a_dtype, bm=256):
    m, d = h.shape
    bm = _tile(m, bm)
    row = pl.BlockSpec((bm, d), lambda i: (i, 0))
    vec = pl.BlockSpec((1, d), lambda i: (0, 0))
    if g_pre is None:
        return pl.pallas_call(
            _resid_kernel, grid=(m // bm,), in_specs=[row, row, vec], out_specs=row,
            out_shape=jax.ShapeDtypeStruct((m, d), F32),
            compiler_params=_cparams("parallel"), name="resid",
        )(h, m_out, g_post.reshape(1, d)), None
    return pl.pallas_call(
        _resid_norm_kernel, grid=(m // bm,), in_specs=[row, row, vec, vec],
        out_specs=[row, row],
        out_shape=[jax.ShapeDtypeStruct((m, d), F32), jax.ShapeDtypeStruct((m, d), a_dtype)],
        compiler_params=_cparams("parallel"), name="resid_norm",
    )(h, m_out, g_post.reshape(1, d), g_pre.reshape(1, d))


def _mm_kernel(*refs, n_w, n_vec, epilogue, m_split):
    x_ref = refs[0]
    w_refs = refs[1:1 + n_w]
    vec_refs = refs[1 + n_w:1 + n_w + n_vec]
    out_refs = refs[1 + n_w + n_vec:]
    ws = [w[...].astype(BF16) for w in w_refs]
    vecs = [v[...] for v in vec_refs]
    part = x_ref.shape[0] // m_split
    for p in range(m_split):
        rows = slice(p * part, (p + 1) * part)
        accs = [jnp.dot(x_ref[rows, :], w, preferred_element_type=F32) for w in ws]
        outs = epilogue(*accs, *vecs)
        for o_ref, val in zip(out_refs, outs):
            o_ref[rows, :] = val.astype(o_ref.dtype)


def _matmul(x, w, layer, col_starts, n_cols, epilogue, out_dtypes, vecs=(), bm=1024, bn=256, x_buffers=2,
            m_split=1, name="matmul"):
    m, kdim = x.shape
    bm = _tile(m, bm)
    bn = _tile(n_cols, bn)
    x_spec = pl.BlockSpec((bm, kdim), lambda i, j: (i, 0), pipeline_mode=pl.Buffered(x_buffers))
    w_specs = []
    for s in col_starts:
        assert s % bn == 0
        w_specs.append(pl.BlockSpec((None, kdim, bn),
                                    functools.partial(lambda i, j, off: (layer, 0, j + off), off=s // bn)))
    vec_spec = pl.BlockSpec((1, bn), lambda i, j: (0, j))
    out_spec = pl.BlockSpec((bm, bn), lambda i, j: (i, j))
    outs = pl.pallas_call(
        functools.partial(_mm_kernel, n_w=len(col_starts), n_vec=len(vecs), epilogue=epilogue, m_split=m_split),
        grid=(m // bm, n_cols // bn),
        in_specs=[x_spec] + w_specs + [vec_spec] * len(vecs),
        out_specs=[out_spec] * len(out_dtypes),
        out_shape=[jax.ShapeDtypeStruct((m, n_cols), dt) for dt in out_dtypes],
        compiler_params=_cparams("parallel", "arbitrary"), name=name,
    )(x, *([w] * len(col_starts)), *[v.reshape(1, n_cols) for v in vecs])
    return outs


def _mm_rows_kernel(*refs, n_rin, n_rvec, n_w, n_vec, n_rout, row_fn, epilogue, n_tiles, n_sub, rblk):
    rin = refs[:n_rin]
    rvec = refs[n_rin:n_rin + n_rvec]
    w_refs = refs[n_rin + n_rvec:n_rin + n_rvec + n_w]
    vec_refs = refs[n_rin + n_rvec + n_w:n_rin + n_rvec + n_w + n_vec]
    outs_at = n_rin + n_rvec + n_w + n_vec
    rout = refs[outs_at:outs_at + n_rout]
    out_refs = refs[outs_at + n_rout:-2]
    x_even_ref, x_odd_ref = refs[-2:]
    i = pl.program_id(0)
    j = pl.program_id(1)

    def step(x_read_ref, x_write_ref):
        vals = row_fn(*[r[...] for r in rin], *[v[...] for v in rvec])
        start = pl.multiple_of(jnp.minimum(j, n_sub - 1) * rblk, rblk)
        x_write_ref[pl.ds(start, rblk), :] = vals[0].astype(x_write_ref.dtype)
        for o_ref, val in zip(rout, vals[1:]):
            o_ref[...] = val.astype(o_ref.dtype)
        if x_read_ref is not None:
            x = x_read_ref[...]
            accs = [jnp.dot(x, w[...].astype(BF16), preferred_element_type=F32) for w in w_refs]
            outs = epilogue(*accs, *[v[...] for v in vec_refs])
            for o_ref, val in zip(out_refs, outs):
                o_ref[...] = val.astype(o_ref.dtype)

    @pl.when(i == 0)
    def _():
        step(None, x_even_ref)

    @pl.when((i > 0) & (i % 2 == 0))
    def _():
        step(x_odd_ref, x_even_ref)

    @pl.when(i % 2 == 1)
    def _():
        step(x_even_ref, x_odd_ref)


def _matmul_rows(row_fn, row_ins, row_vecs, row_out_dtypes, w, layer, col_starts, n_cols, epilogue, out_dtypes,
                 vecs=(), bm=1024, bn=256, name="matmul_rows"):
    m, kdim = row_ins[0].shape
    bm = _tile(m, bm)
    bn = _tile(n_cols, bn)
    n_tiles, n_j = m // bm, n_cols // bn
    n_sub = 1
    while n_sub * 2 <= n_j and bm % (n_sub * 2) == 0 and (bm // (n_sub * 2)) % 16 == 0:
        n_sub *= 2
    rblk = bm // n_sub
    last = n_tiles * n_sub - 1

    def row_idx(i, j):
        return (jnp.minimum(i * n_sub + jnp.minimum(j, n_sub - 1), last), 0)

    def col(i, j):
        return j * jnp.minimum(i, 1)

    row_spec = pl.BlockSpec((rblk, kdim), row_idx)
    rvec_spec = pl.BlockSpec((1, kdim), lambda i, j: (0, 0))
    w_specs = []
    for s in col_starts:
        assert s % bn == 0
        w_specs.append(pl.BlockSpec((None, kdim, bn),
                                    functools.partial(lambda i, j, off: (layer, 0, col(i, j) + off), off=s // bn)))
    vec_spec = pl.BlockSpec((1, bn), lambda i, j: (0, col(i, j)))
    out_spec = pl.BlockSpec((bm, bn), lambda i, j: (jnp.maximum(i - 1, 0), col(i, j)))
    outs = pl.pallas_call(
        functools.partial(_mm_rows_kernel, n_rin=len(row_ins), n_rvec=len(row_vecs), n_w=len(col_starts),
                          n_vec=len(vecs), n_rout=len(row_out_dtypes), row_fn=row_fn, epilogue=epilogue,
                          n_tiles=n_tiles, n_sub=n_sub, rblk=rblk),
        grid=(n_tiles + 1, n_j),
        in_specs=[row_spec] * len(row_ins) + [rvec_spec] * len(row_vecs) + w_specs + [vec_spec] * len(vecs),
        out_specs=[row_spec] * len(row_out_dtypes) + [out_spec] * len(out_dtypes),
        out_shape=([jax.ShapeDtypeStruct((m, kdim), dt) for dt in row_out_dtypes]
                   + [jax.ShapeDtypeStruct((m, n_cols), dt) for dt in out_dtypes]),
        scratch_shapes=[pltpu.VMEM((bm, kdim), BF16), pltpu.VMEM((bm, kdim), BF16)],
        compiler_params=_cparams("arbitrary", "arbitrary"), name=name,
    )(*row_ins, *[v.reshape(1, kdim) for v in row_vecs], *([w] * len(col_starts)),
      *[v.reshape(1, n_cols) for v in vecs])
    return outs


def _rows_resid_norm(h, m_out, g_post, g_pre):
    h_new = h + _rms(m_out.astype(F32), g_post)
    a = _rms(h_new, g_pre)
    return a, h_new, a


def _ep_identity(acc):
    return (acc,)


def _ep_silu(acc):
    return (acc * jax.nn.sigmoid(acc),)


def _ep_glu(val, gate):
    return (val * jax.nn.sigmoid(gate),)


def _ep_swiglu(gate, up):
    return (gate * jax.nn.sigmoid(gate) * up,)


def _ep_forget(f, lb):
    e = jnp.exp(-jnp.abs(f))
    one_e = 1.0 + e
    log_sig = jnp.minimum(f, 0.0) - jnp.log(one_e)
    a = jnp.log(lb)
    b = jnp.log1p(-lb) + log_sig
    log_forget = jnp.maximum(a, b) + jnp.log(1.0 + jnp.exp(-jnp.abs(a - b)))
    key = (1.0 - lb) * jnp.where(f >= 0.0, e, 1.0) / one_e
    return log_forget, key


def _ep_hgrn_fvg(f, v, g, lb):
    return _ep_forget(f, lb) + (v,) + _ep_silu(g)


def _s5_prep_kernel(are_ref, aim_ref, ldt_ref, btr_ref, bti_ref, cre_ref, cim_ref,
                    dt_ref, bac_ref, cpc_ref, al_ref):
    gb = are_ref.shape[0]
    h, big_l = S5_GROUP, S5_CHUNK
    lane = lax.broadcasted_iota(jnp.int32, (h, LANES), 1)
    for g in range(gb):
        rows = slice(g * h, (g + 1) * h)
        lam_re = jnp.minimum(are_ref[g], -S5_EIG_CLIP)
        lam_im = aim_ref[g]
        dt = jnp.exp(ldt_ref[g])
        mag = jnp.exp(lam_re * dt)
        ar = mag * jnp.cos(lam_im * dt)
        ai = mag * jnp.sin(lam_im * dt)
        den = lam_re * lam_re + lam_im * lam_im
        z_re = ((ar - 1.0) * lam_re + ai * lam_im) / den
        z_im = (ai * lam_re - (ar - 1.0) * lam_im) / den
        btr, bti = btr_ref[g], bti_ref[g]
        bbr = z_re * btr - z_im * bti
        bbi = z_re * bti + z_im * btr
        cre, cim = cre_ref[g], cim_ref[g]
        pr = jnp.ones_like(ar)
        pi = jnp.zeros_like(ar)
        cp_rows = []
        for tau in range(big_l + 1):
            cp = jnp.concatenate([cre * pr - cim * pi, -(cre * pi + cim * pr)], axis=-1)
            if tau < big_l:
                cp_rows.append(cp)
                ba = jnp.concatenate([bbr * pr - bbi * pi, bbr * pi + bbi * pr], axis=-1)
                bac_ref[tau, rows, :] = ba.astype(bac_ref.dtype)
            if tau >= 1:
                cpc_ref[tau - 1, rows, :] = cp.astype(cpc_ref.dtype)
            if tau == big_l:
                al_ref[0, g:g + 1, :] = jnp.concatenate([pr, pr], axis=-1)
                al_ref[1, g:g + 1, :] = jnp.concatenate([-pi, pi], axis=-1)
            pr, pi = pr * ar - pi * ai, pr * ai + pi * ar
        cp_all = jnp.concatenate(cp_rows, axis=0)
        bb = jnp.concatenate([bbr, bbi], axis=-1)
        krow = lax.dot_general(bb, cp_all, (((1,), (1,)), ((), ())), precision=HIGHEST,
                               preferred_element_type=F32)
        own = (lane >= g * h) & (lane < (g + 1) * h)
        taus_per_vreg = LANES // h
        for tau in range(big_l):
            src = krow[:, (tau // taus_per_vreg) * LANES:(tau // taus_per_vreg + 1) * LANES]
            shift = ((g - tau % taus_per_vreg) * h) % LANES
            moved = src if shift == 0 else pltpu.roll(src, shift, axis=1)
            dt_ref[tau, rows, :] = jnp.where(own, moved, 0.0).astype(dt_ref.dtype)


def _s5_prep(a_re, a_im, log_dt, b_re, b_im, c_re, c_im):
    g, p = a_re.shape
    h, big_l = S5_GROUP, S5_CHUNK
    gb = LANES // h
    assert 2 * p == LANES and g % gb == 0
    row = lambda z: z.astype(F32).reshape(g, 1, p)
    ldt = jnp.broadcast_to(log_dt.astype(F32)[:, None, None], (g, 1, p))
    bt = lambda z: jnp.swapaxes(z.astype(F32), 1, 2)
    vspec = pl.BlockSpec((gb, 1, p), lambda i: (i, 0, 0))
    mspec = pl.BlockSpec((gb, h, p), lambda i: (i, 0, 0))
    op_spec = pl.BlockSpec((None, big_l, LANES, LANES), lambda i: (i, 0, 0, 0))
    op_shape = jax.ShapeDtypeStruct((g // gb, big_l, LANES, LANES), BF16)
    return pl.pallas_call(
        _s5_prep_kernel, grid=(g // gb,),
        in_specs=[vspec, vspec, vspec, mspec, mspec, mspec, mspec],
        out_specs=[op_spec, op_spec, op_spec, pl.BlockSpec((None, 2, gb, LANES), lambda i: (i, 0, 0, 0))],
        out_shape=[op_shape, op_shape, op_shape, jax.ShapeDtypeStruct((g // gb, 2, gb, LANES), F32)],
        compiler_params=_cparams("parallel"), name="s5_prep",
    )(row(a_re), row(a_im), ldt, bt(b_re), bt(b_im), c_re.astype(F32), c_im.astype(F32))


def _s5_main_kernel(a_ref, dt_ref, bac_ref, cpc_ref, al_ref, y_ref,
                    u2_ref, bt_ref, bdw_ref, bdo_ref, w3_ref, w3s_ref, sp3_ref, *, batch, n_chunks):
    h, big_l = S5_GROUP, S5_CHUNK
    gb = LANES // h
    nrows = batch * n_chunks
    nseq = batch * gb

    @pl.when(pl.program_id(0) == 0)
    def _():
        bt_ref[...] = jnp.zeros_like(bt_ref)
        bdw_ref[...] = jnp.zeros_like(bdw_ref)
        bdo_ref[...] = jnp.zeros_like(bdo_ref)

    for s in range(big_l):
        for t in range(s, big_l):
            bt_ref[s * LANES:(s + 1) * LANES, t * LANES:(t + 1) * LANES] = dt_ref[t - s]
        for g in range(gb):
            rows = slice(s * LANES + g * h, s * LANES + (g + 1) * h)
            cols = slice(g * LANES, (g + 1) * LANES)
            bdw_ref[rows, cols] = bac_ref[big_l - 1 - s, g * h:(g + 1) * h, :]
            bdo_ref[rows, cols] = cpc_ref[s, g * h:(g + 1) * h, :]

    for s in range(big_l):
        u2_ref[:, s * LANES:(s + 1) * LANES] = a_ref[pl.ds(s, nrows, stride=big_l), :].astype(BF16)
    u2 = u2_ref[...]

    win = jnp.dot(u2, bdw_ref[...], preferred_element_type=F32)
    for b in range(batch):
        for g in range(gb):
            w3_ref[pl.ds(b * gb + g, n_chunks, stride=nseq), :] = (
                win[b * n_chunks:(b + 1) * n_chunks, g * LANES:(g + 1) * LANES])
    w3s_ref[...] = pltpu.roll(w3_ref[...], LANES // 2, axis=1)
    a1 = jnp.concatenate([al_ref[0]] * batch, axis=0)
    a2 = jnp.concatenate([al_ref[1]] * batch, axis=0)

    def step(c, carry):
        st, st_sw = carry
        base = pl.multiple_of(c * nseq, nseq)
        sp3_ref[pl.ds(base, nseq), :] = st
        new = a1 * st + a2 * st_sw + w3_ref[pl.ds(base, nseq), :]
        new_sw = a1 * st_sw - a2 * st + w3s_ref[pl.ds(base, nseq), :]
        return new, new_sw

    zero = jnp.zeros((nseq, LANES), F32)
    lax.fori_loop(0, n_chunks, step, (zero, zero), unroll=True)
    sprev =jnp.concatenate(
        [jnp.concatenate([sp3_ref[pl.ds(b * gb + g, n_chunks, stride=nseq), :] for b in range(batch)], axis=0)
         for g in range(gb)], axis=1).astype(BF16)

    nt = (((1,), (1,)), ((), ()))
    pair = 2 * LANES
    for tp in range(big_l * LANES // pair):
        cols = slice(tp * pair, (tp + 1) * pair)
        live = (tp + 1) * pair
        y2 = (jnp.dot(u2_ref[:, 0:live], bt_ref[0:live, cols], preferred_element_type=F32)
              + lax.dot_general(sprev, bdo_ref[cols, :], nt, preferred_element_type=F32))
        y_ref[pl.ds(2 * tp, nrows, stride=big_l), :] = y2[:, 0:LANES]
        y_ref[pl.ds(2 * tp + 1, nrows, stride=big_l), :] = y2[:, LANES:pair]


def _s5_core(a, batch, dt_op, bac, cpc, al):
    m, d = a.shape
    t = m // batch
    big_l = S5_CHUNK
    gb = LANES // S5_GROUP
    n_chunks = t // big_l
    nrows = batch * n_chunks
    width = big_l * LANES
    tok = pl.BlockSpec((m, LANES), lambda i: (0, i))
    op_spec = pl.BlockSpec((None, big_l, LANES, LANES), lambda i: (i, 0, 0, 0))
    return pl.pallas_call(
        functools.partial(_s5_main_kernel, batch=batch, n_chunks=n_chunks),
        grid=(d // LANES,),
        in_specs=[tok, op_spec, op_spec, op_spec, pl.BlockSpec((None, 2, gb, LANES), lambda i: (i, 0, 0, 0))],
        out_specs=tok,
        out_shape=jax.ShapeDtypeStruct((m, d), F32),
        scratch_shapes=[pltpu.VMEM((nrows, width), BF16),
                        pltpu.VMEM((width, width), BF16),
                        pltpu.VMEM((width, gb * LANES), BF16),
                        pltpu.VMEM((width, gb * LANES), BF16),
                        pltpu.VMEM((n_chunks * batch * gb, LANES), F32),
                        pltpu.VMEM((n_chunks * batch * gb, LANES), F32),
                        pltpu.VMEM((n_chunks * batch * gb, LANES), F32)],
        compiler_params=_cparams("arbitrary"), name="s5_main",
    )(a, dt_op, bac, cpc, al)


def _s5_act_kernel(y_ref, a_ref, d_ref, z_ref):
    z_ref[...] = jax.nn.gelu(y_ref[...] + d_ref[...] * a_ref[...]).astype(z_ref.dtype)


def _s5_act(y, a, d_skip, bm=256):
    m, d = y.shape
    bm = _tile(m, bm)
    row = pl.BlockSpec((bm, d), lambda i: (i, 0))
    vec = pl.BlockSpec((1, d), lambda i: (0, 0))
    return pl.pallas_call(
        _s5_act_kernel, grid=(m // bm,), in_specs=[row, row, vec], out_specs=row,
        out_shape=jax.ShapeDtypeStruct((m, d), BF16),
        compiler_params=_cparams("parallel"), name="s5_act",
    )(y, a, d_skip.astype(F32).reshape(1, d))


def _hgrn_stage1(q, k, cum, crow, vb, st, diag_mask, with_diag):
    nt = (((1,), (1,)), ((), ()))
    tn = (((0,), (0,)), ((), ()))
    r15, r31, r47, r63 = crow(15), crow(31), crow(47), crow(63)

    def zrows(n):
        return jnp.zeros((n, HGRN_HEAD), F32)

    q0 = q * jnp.exp(cum)
    q15 = q[16:32] * jnp.exp(cum[16:32] - r15)
    q31 = q[32:64] * jnp.exp(cum[32:64] - r31)
    q47 = q[48:64] * jnp.exp(cum[48:64] - r47)
    k63 = k * jnp.exp(r63 - cum)
    k31 = k[0:32] * jnp.exp(r31 - cum[0:32])
    k15 = k[0:16] * jnp.exp(r15 - cum[0:16])
    k47 = k[32:48] * jnp.exp(r47 - cum[32:48])
    o_state = lax.dot_general(q0.astype(BF16), st.astype(BF16), nt, preferred_element_type=F32)
    st_new = st * jnp.exp(r63) + lax.dot_general(vb, k63.astype(BF16), tn, preferred_element_type=F32)
    q_cat = jnp.concatenate([
        jnp.concatenate([zrows(32), q31], axis=0),
        jnp.concatenate([zrows(16), q15, zrows(32)], axis=0),
        jnp.concatenate([zrows(48), q47], axis=0)], axis=-1).astype(BF16)
    k_cat = jnp.concatenate([
        jnp.concatenate([k31, zrows(32)], axis=0),
        jnp.concatenate([k15, zrows(48)], axis=0),
        jnp.concatenate([zrows(32), k47, zrows(16)], axis=0)], axis=-1).astype(BF16)
    scores = lax.dot_general(q_cat, k_cat, nt, preferred_element_type=F32)
    if with_diag:
        q_d = jnp.concatenate([q0[0:16], q15, q31[0:16], q47], axis=0).astype(BF16)
        k_d = jnp.concatenate([
            k[0:16] * jnp.exp(jnp.minimum(-cum[0:16], HGRN_SAFE_DECAY)),
            k[16:32] * jnp.exp(jnp.minimum(r15 - cum[16:32], HGRN_SAFE_DECAY)),
            k[32:48] * jnp.exp(jnp.minimum(r31 - cum[32:48], HGRN_SAFE_DECAY)),
            k[48:64] * jnp.exp(jnp.minimum(r47 - cum[48:64], HGRN_SAFE_DECAY))], axis=0).astype(BF16)
        scores = scores + jnp.where(diag_mask, lax.dot_general(q_d, k_d, nt, preferred_element_type=F32), 0.0)
    return o_state, st_new, scores


def _hgrn_exact_diag(qtile, krow, crow):
    c_len, sub = HGRN_CHUNK, HGRN_SUB
    row8 = lax.broadcasted_iota(jnp.int32, (SUBLANES, HGRN_HEAD), 0)
    lane8 = lax.broadcasted_iota(jnp.int32, (SUBLANES, HGRN_HEAD), 1)
    pairs = [(t0, list(range((t0 // sub) * sub, t0 + SUBLANES))) for t0 in range(0, c_len, SUBLANES)]
    e_tiles = []
    for t0, s_list in pairs:
        qt = qtile(t0)
        ct = jnp.concatenate([crow(t0 + i) for i in range(SUBLANES)], axis=0)
        for s in s_list:
            e = qt * krow(s) * jnp.exp(ct - crow(s))
            if s >= t0:
                e = jnp.where(row8 >= s - t0, e, 0.0)
            e_tiles.append(e)
    e_all = jnp.concatenate(e_tiles, axis=0).astype(BF16)
    sums = jnp.dot(e_all, jnp.ones((HGRN_HEAD, HGRN_HEAD), BF16), preferred_element_type=F32)
    diag_tiles = []
    idx = 0
    for t0, s_list in pairs:
        acc = jnp.zeros((SUBLANES, HGRN_HEAD), F32)
        for s in s_list:
            acc = jnp.where(lane8 == s, sums[idx * SUBLANES:(idx + 1) * SUBLANES], acc)
            idx += 1
        diag_tiles.append(acc)
    return jnp.concatenate(diag_tiles, axis=0)[:, 0:c_len]


def _hgrn_finish(o, scores, vb, gn, sg):
    o = o + jnp.dot(scores.astype(BF16), vb, preferred_element_type=F32)
    o = o * lax.rsqrt(jnp.mean(o * o, axis=-1, keepdims=True) + RMS_EPS)
    return o * gn * sg.astype(F32)


def _hgrn_core_kernel(q_ref, k_ref, lf_ref, v_ref, sg_ref, gn_ref, o_ref, st_ref, cum_ref, kc_ref, cc_ref,
                      tri_ref, *, heads):
    c_len, sub, hd_dim = HGRN_CHUNK, HGRN_SUB, HGRN_HEAD
    tb = q_ref.shape[0]
    n_chunks = tb // c_len
    assert c_len == 4 * sub

    ri = lax.broadcasted_iota(jnp.int32, (c_len, c_len), 0)
    ci = lax.broadcasted_iota(jnp.int32, (c_len, c_len), 1)
    diag_mask = (ri >= ci) & ((ri // sub) == (ci // sub))
    lanes = [slice(hd * hd_dim, (hd + 1) * hd_dim) for hd in range(heads)]

    @pl.when(pl.program_id(2) == 0)
    def _():
        st_ref[...] = jnp.zeros_like(st_ref)
        rb = lax.broadcasted_iota(jnp.int32, (tb, tb), 0)
        cb = lax.broadcasted_iota(jnp.int32, (tb, tb), 1)
        tri = ((rb >= cb) & ((rb // c_len) == (cb // c_len))).astype(BF16)
        tri_ref[...] = jnp.concatenate([tri, tri, tri], axis=1)

    lf = lf_ref[...]
    lf_hi = lf.astype(BF16)
    rest = lf - lf_hi.astype(F32)
    lf_mid = rest.astype(BF16)
    lf_lo = (rest - lf_mid.astype(F32)).astype(BF16)
    cum_ref[...] = jnp.dot(tri_ref[...], jnp.concatenate([lf_hi, lf_mid, lf_lo], axis=0),
                           preferred_element_type=F32)
    block_decay = -jnp.sum(lf.reshape(tb // sub, sub, lf.shape[1]), axis=1)
    exact_needed = jnp.max(block_decay) > HGRN_SAFE_DECAY

    @pl.when(jnp.logical_not(exact_needed))
    def _():
        for c in range(n_chunks):
            r0 = c * c_len
            rows = slice(r0, r0 + c_len)
            for hd, ls in enumerate(lanes):
                vb = v_ref[rows, ls].astype(BF16)
                o, st_new, scores = _hgrn_stage1(
                    q_ref[rows, ls], k_ref[rows, ls], cum_ref[rows, ls],
                    lambda i, r0=r0, ls=ls: cum_ref[r0 + i:r0 + i + 1, ls],
                    vb, st_ref[hd], diag_mask, True)
                st_ref[hd] = st_new
                o_ref[rows, ls] = _hgrn_finish(o, scores, vb, gn_ref[:, ls], sg_ref[rows, ls]).astype(o_ref.dtype)

    @pl.when(exact_needed)
    def _():
        def chunk_body(c, carry):
            r0 = pl.multiple_of(c * c_len, c_len)
            rows = pl.ds(r0, c_len)
            cc_ref[...] = cum_ref[rows, :]
            kc_ref[...] = k_ref[rows, :]
            for hd, ls in enumerate(lanes):
                vb = v_ref[rows, ls].astype(BF16)
                o, st_new, scores = _hgrn_stage1(
                    q_ref[rows, ls], kc_ref[:, ls], cc_ref[:, ls], lambda i, ls=ls: cc_ref[i:i + 1, ls],
                    vb, st_ref[hd], diag_mask, False)
                st_ref[hd] = st_new
                scores = scores + _hgrn_exact_diag(
                    lambda t0, ls=ls: q_ref[pl.ds(r0 + t0, SUBLANES), ls],
                    lambda s, ls=ls: kc_ref[s:s + 1, ls],
                    lambda s, ls=ls: cc_ref[s:s + 1, ls])
                o_ref[rows, ls] = _hgrn_finish(o, scores, vb, gn_ref[:, ls], sg_ref[rows, ls]).astype(o_ref.dtype)
            return carry

        lax.fori_loop(0, n_chunks, chunk_body, 0)


def _hgrn_core(q, k, log_f, v, sg, g_norm, batch, tb=256, heads=16):
    m, d = q.shape
    t = m // batch
    tb = _tile(t, tb)
    heads = _tile(d // HGRN_HEAD, heads)
    wl = heads * HGRN_HEAD
    nt = t // tb
    blk = pl.BlockSpec((tb, wl), lambda b, j, i: (b * nt + i, j))
    vec = pl.BlockSpec((1, wl), lambda b, j, i: (0, j))
    return pl.pallas_call(
        functools.partial(_hgrn_core_kernel, heads=heads),
        grid=(batch, d // wl, nt),
        in_specs=[blk, blk, blk, blk, blk, vec],
        out_specs=blk,
        out_shape=jax.ShapeDtypeStruct((m, d), BF16),
        scratch_shapes=[pltpu.VMEM((heads, HGRN_HEAD, HGRN_HEAD), F32),
                        pltpu.VMEM((tb, wl), F32),
                        pltpu.VMEM((HGRN_CHUNK, wl), F32),
                        pltpu.VMEM((HGRN_CHUNK, wl), F32),
                        pltpu.VMEM((tb, 3 * tb), BF16)],
        compiler_params=_cparams("parallel", "parallel", "arbitrary"), name="hgrn_core",
    )(q, k, log_f, v, sg, g_norm.astype(F32).reshape(1, d))


def kernel(x, norm_gains, s5_a_re, s5_a_im, s5_log_dt, s5_b_re, s5_b_im, s5_c_re, s5_c_im, s5_d,
           s5_w_glu, hgrn_w_in, hgrn_lb_logits, hgrn_g_norm, hgrn_w_out, ffn_w_gate_up, ffn_w_down):
    batch, seq, d = x.shape
    depth = norm_gains.shape[0]
    d_ff = ffn_w_down.shape[1]
    m = batch * seq
    gains = norm_gains.astype(F32)
    lower_bounds = _lower_bounds(hgrn_lb_logits)

    h = x.reshape(m, d).astype(F32)
    a = _norm(h, gains[0, 0], F32)
    f_out = None
    for layer in range(depth):
        j = layer // 2
        if layer % 2 == 0:
            if f_out is not None:
                h, a = _resid_norm(h, f_out, gains[layer - 1, 3], gains[layer, 0], F32)
            s5_ops = _s5_prep(s5_a_re[j], s5_a_im[j], s5_log_dt[j], s5_b_re[j], s5_b_im[j],
                              s5_c_re[j], s5_c_im[j])
            y = _s5_core(a, batch, *s5_ops)
            z = _s5_act(y, a, s5_d[j])
            (mix,) = _matmul(z, s5_w_glu, j, (0, d), d, _ep_glu, (BF16,), bm=2048, name="s5_glu")
        else:
            h, a, q = _matmul_rows(_rows_resid_norm, (h, f_out), (gains[layer - 1, 3], gains[layer, 0]),
                                   (F32, BF16), hgrn_w_in, j, (0,), d, _ep_silu, (F32,), bn=512, name="hgrn_q")
            log_f, key, val, sg = _matmul(a, hgrn_w_in, j, (d, 2 * d, 3 * d), d, _ep_hgrn_fvg, (F32, F32, F32, BF16),
                                          vecs=(lower_bounds[layer],), m_split=2, name="hgrn_fvg")
            o = _hgrn_core(q, key, log_f, val, sg, hgrn_g_norm[j], batch)
            (mix,) = _matmul(o, hgrn_w_out, j, (0,), d, _ep_identity, (BF16,), bn=512, name="hgrn_out")
        h, a_ffn = _resid_norm(h, mix, gains[layer, 1], gains[layer, 2], BF16)
        (act,) = _matmul(a_ffn, ffn_w_gate_up, layer, (0, d_ff), d_ff, _ep_swiglu, (BF16,), bm=2048, name="ffn_up")
        (f_out,) = _matmul(act, ffn_w_down, layer, (0,), d, _ep_identity, (BF16,), x_buffers=1, name="ffn_down")
    h, _ = _resid_norm(h, f_out, gains[depth - 1, 3], None, None)
    return h.reshape(batch, seq, d).astype(x.dtype)
```

```python
import functools

import jax
import jax.numpy as jnp
from jax import lax
from jax.experimental import pallas as pl
from jax.experimental.pallas import tpu as pltpu

F32 = jnp.float32
BF16 = jnp.bfloat16
HIGHEST = lax.Precision.HIGHEST

RMS_EPS = 1e-6
S5_GROUP = 16
S5_STATE = 64
S5_EIG_CLIP = 1e-4
S5_CHUNK = 16
HGRN_HEAD = 128
HGRN_CHUNK = 64
HGRN_SUB = 16
HGRN_SAFE_DECAY = 60.0
LANES = 128
SUBLANES = 8
BF16_ROWS = 2 * SUBLANES
VMEM_LIMIT = 56 * 1024 * 1024


def _cparams(*sem):
    return pltpu.CompilerParams(dimension_semantics=sem, vmem_limit_bytes=VMEM_LIMIT)


def _tile(n, want):
    t = min(n, want)
    assert n % t == 0, (n, want)
    return t


def _lower_bounds_kernel(logit_ref, out_ref):
    x = logit_ref[...]
    e = jnp.exp(x - jnp.max(x, axis=0, keepdims=True))
    p = e / jnp.sum(e, axis=0, keepdims=True)
    acc = jnp.zeros_like(p[0:1])
    rows = [acc]
    for layer in range(1, x.shape[0]):
        acc = acc + p[layer:layer + 1]
        rows.append(acc)
    out_ref[...] = jnp.concatenate(rows, axis=0)


def _lower_bounds(logits):
    return pl.pallas_call(
        _lower_bounds_kernel,
        out_shape=jax.ShapeDtypeStruct(logits.shape, F32),
        name="lower_bounds",
    )(logits.astype(F32))


def _rms(x, gain):
    return x * lax.rsqrt(jnp.mean(x * x, axis=-1, keepdims=True) + RMS_EPS) * gain


def _norm_kernel(x_ref, g_ref, o_ref):
    o_ref[...] = _rms(x_ref[...], g_ref[...]).astype(o_ref.dtype)


def _norm(x, gain, out_dtype, bm=256):
    m, d = x.shape
    bm = _tile(m, bm)
    row = pl.BlockSpec((bm, d), lambda i: (i, 0))
    vec = pl.BlockSpec((1, d), lambda i: (0, 0))
    return pl.pallas_call(
        _norm_kernel, grid=(m // bm,), in_specs=[row, vec], out_specs=row,
        out_shape=jax.ShapeDtypeStruct((m, d), out_dtype),
        compiler_params=_cparams("parallel"), name="rms_norm",
    )(x, gain.reshape(1, d))


def _resid_norm_kernel(h_ref, m_ref, gpost_ref, gpre_ref, h_out_ref, a_out_ref):
    h = h_ref[...] + _rms(m_ref[...].astype(F32), gpost_ref[...])
    h_out_ref[...] = h
    a_out_ref[...] = _rms(h, gpre_ref[...]).astype(a_out_ref.dtype)


def _resid_kernel(h_ref, m_ref, gpost_ref, h_out_ref):
    h_out_ref[...] = h_ref[...] + _rms(m_ref[...].astype(F32), gpost_ref[...])


def _resid_norm(h, m_out, g_post, g_pre, a_dtype, bm=256):
    m, d = h.shape
    bm = _tile(m, bm)
    row = pl.BlockSpec((bm, d), lambda i: (i, 0))
    vec = pl.BlockSpec((1, d), lambda i: (0, 0))
    if g_pre is None:
        return pl.pallas_call(
            _resid_kernel, grid=(m // bm,), in_specs=[row, row, vec], out_specs=row,
            out_shape=jax.ShapeDtypeStruct((m, d), F32),
            compiler_params=_cparams("parallel"), name="resid",
        )(h, m_out, g_post.reshape(1, d)), None
    return pl.pallas_call(
        _resid_norm_kernel, grid=(m // bm,), in_specs=[row, row, vec, vec],
        out_specs=[row, row],
        out_shape=[jax.ShapeDtypeStruct((m, d), F32), jax.ShapeDtypeStruct((m, d), a_dtype)],
        compiler_params=_cparams("parallel"), name="resid_norm",
    )(h, m_out, g_post.reshape(1, d), g_pre.reshape(1, d))


def _mm_kernel(*refs, n_w, n_vec, epilogue, m_split):
    x_ref = refs[0]
    w_refs = refs[1:1 + n_w]
    vec_refs = refs[1 + n_w:1 + n_w + n_vec]
    out_refs = refs[1 + n_w + n_vec:]
    ws = [w[...].astype(BF16) for w in w_refs]
    vecs = [v[...] for v in vec_refs]
    part = x_ref.shape[0] // m_split
    for p in range(m_split):
        rows = slice(p * part, (p + 1) * part)
        accs = [jnp.dot(x_ref[rows, :], w, preferred_element_type=F32) for w in ws]
        outs = epilogue(*accs, *vecs)
        for o_ref, val in zip(out_refs, outs):
            o_ref[rows, :] = val.astype(o_ref.dtype)


def _matmul(x, w, layer, col_starts, n_cols, epilogue, out_dtypes, vecs=(), bm=1024, bn=256, x_buffers=2,
            m_split=1, name="matmul"):
    m, kdim = x.shape
    bm = _tile(m, bm)
    bn = _tile(n_cols, bn)
    x_spec = pl.BlockSpec((bm, kdim), lambda i, j: (i, 0), pipeline_mode=pl.Buffered(x_buffers))
    w_specs = []
    for s in col_starts:
        assert s % bn == 0
        w_specs.append(pl.BlockSpec((None, kdim, bn),
                                    functools.partial(lambda i, j, off: (layer, 0, j + off), off=s // bn)))
    vec_spec = pl.BlockSpec((1, bn), lambda i, j: (0, j))
    out_spec = pl.BlockSpec((bm, bn), lambda i, j: (i, j))
    outs = pl.pallas_call(
        functools.partial(_mm_kernel, n_w=len(col_starts), n_vec=len(vecs), epilogue=epilogue, m_split=m_split),
        grid=(m // bm, n_cols // bn),
        in_specs=[x_spec] + w_specs + [vec_spec] * len(vecs),
        out_specs=[out_spec] * len(out_dtypes),
        out_shape=[jax.ShapeDtypeStruct((m, n_cols), dt) for dt in out_dtypes],
        compiler_params=_cparams("parallel", "arbitrary"), name=name,
    )(x, *([w] * len(col_starts)), *[v.reshape(1, n_cols) for v in vecs])
    return outs


def _mm_rows_kernel(*refs, n_rin, n_rvec, n_w, n_vec, n_rout, row_fn, epilogue, n_tiles, n_sub, rblk):
    rin = refs[:n_rin]
    rvec = refs[n_rin:n_rin + n_rvec]
    w_refs = refs[n_rin + n_rvec:n_rin + n_rvec + n_w]
    vec_refs = refs[n_rin + n_rvec + n_w:n_rin + n_rvec + n_w + n_vec]
    outs_at = n_rin + n_rvec + n_w + n_vec
    rout = refs[outs_at:outs_at + n_rout]
    out_refs = refs[outs_at + n_rout:-2]
    x_even_ref, x_odd_ref = refs[-2:]
    i = pl.program_id(0)
    j = pl.program_id(1)

    def step(x_read_ref, x_write_ref):
        vals = row_fn(*[r[...] for r in rin], *[v[...] for v in rvec])
        start = pl.multiple_of(jnp.minimum(j, n_sub - 1) * rblk, rblk)
        x_write_ref[pl.ds(start, rblk), :] = vals[0].astype(x_write_ref.dtype)
        for o_ref, val in zip(rout, vals[1:]):
            o_ref[...] = val.astype(o_ref.dtype)
        if x_read_ref is not None:
            x = x_read_ref[...]
            accs = [jnp.dot(x, w[...].astype(BF16), preferred_element_type=F32) for w in w_refs]
            outs = epilogue(*accs, *[v[...] for v in vec_refs])
            for o_ref, val in zip(out_refs, outs):
                o_ref[...] = val.astype(o_ref.dtype)

    @pl.when(i == 0)
    def _():
        step(None, x_even_ref)

    @pl.when((i > 0) & (i % 2 == 0))
    def _():
        step(x_odd_ref, x_even_ref)

    @pl.when(i % 2 == 1)
    def _():
        step(x_even_ref, x_odd_ref)


def _matmul_rows(row_fn, row_ins, row_vecs, row_out_dtypes, w, layer, col_starts, n_cols, epilogue, out_dtypes,
                 vecs=(), bm=1024, bn=256, name="matmul_rows"):
    m, kdim = row_ins[0].shape
    bm = _tile(m, bm)
    bn = _tile(n_cols, bn)
    n_tiles, n_j = m // bm, n_cols // bn
    n_sub = 1
    while n_sub * 2 <= n_j and bm % (n_sub * 2) == 0 and (bm // (n_sub * 2)) % BF16_ROWS == 0:
        n_sub *= 2
    rblk = bm // n_sub
    last = n_tiles * n_sub - 1

    def row_idx(i, j):
        return (jnp.minimum(i * n_sub + jnp.minimum(j, n_sub - 1), last), 0)

    def col(i, j):
        return j * jnp.minimum(i, 1)

    row_spec = pl.BlockSpec((rblk, kdim), row_idx)
    rvec_spec = pl.BlockSpec((1, kdim), lambda i, j: (0, 0))
    w_specs = []
    for s in col_starts:
        assert s % bn == 0
        w_specs.append(pl.BlockSpec((None, kdim, bn),
                                    functools.partial(lambda i, j, off: (layer, 0, col(i, j) + off), off=s // bn)))
    vec_spec = pl.BlockSpec((1, bn), lambda i, j: (0, col(i, j)))
    out_spec = pl.BlockSpec((bm, bn), lambda i, j: (jnp.maximum(i - 1, 0), col(i, j)))
    outs = pl.pallas_call(
        functools.partial(_mm_rows_kernel, n_rin=len(row_ins), n_rvec=len(row_vecs), n_w=len(col_starts),
                          n_vec=len(vecs), n_rout=len(row_out_dtypes), row_fn=row_fn, epilogue=epilogue,
                          n_tiles=n_tiles, n_sub=n_sub, rblk=rblk),
        grid=(n_tiles + 1, n_j),
        in_specs=[row_spec] * len(row_ins) + [rvec_spec] * len(row_vecs) + w_specs + [vec_spec] * len(vecs),
        out_specs=[row_spec] * len(row_out_dtypes) + [out_spec] * len(out_dtypes),
        out_shape=([jax.ShapeDtypeStruct((m, kdim), dt) for dt in row_out_dtypes]
                   + [jax.ShapeDtypeStruct((m, n_cols), dt) for dt in out_dtypes]),
        scratch_shapes=[pltpu.VMEM((bm, kdim), BF16), pltpu.VMEM((bm, kdim), BF16)],
        compiler_params=_cparams("arbitrary", "arbitrary"), name=name,
    )(*row_ins, *[v.reshape(1, kdim) for v in row_vecs], *([w] * len(col_starts)),
      *[v.reshape(1, n_cols) for v in vecs])
    return outs


def _rows_resid_norm(h, m_out, g_post, g_pre):
    h_new = h + _rms(m_out.astype(F32), g_post)
    a = _rms(h_new, g_pre)
    return a, h_new, a


def _ep_identity(acc):
    return (acc,)


def _ep_silu(acc):
    return (acc * jax.nn.sigmoid(acc),)


def _ep_glu(val, gate):
    return (val * jax.nn.sigmoid(gate),)


def _ep_swiglu(gate, up):
    return (gate * jax.nn.sigmoid(gate) * up,)


def _ep_forget(f, lb):
    e = jnp.exp(-jnp.abs(f))
    one_e = 1.0 + e
    log_sig = jnp.minimum(f, 0.0) - jnp.log(one_e)
    a = jnp.log(lb)
    b = jnp.log1p(-lb) + log_sig
    log_forget = jnp.maximum(a, b) + jnp.log(1.0 + jnp.exp(-jnp.abs(a - b)))
    key = (1.0 - lb) * jnp.where(f >= 0.0, e, 1.0) / one_e
    return log_forget, key


def _ep_hgrn_fvg(f, v, g, lb):
    return _ep_forget(f, lb) + (v,) + _ep_silu(g)


def _s5_prep_kernel(are_ref, aim_ref, ldt_ref, btr_ref, bti_ref, cre_ref, cim_ref,
                    dt_ref, bac_ref, cpc_ref, al_ref):
    gb = are_ref.shape[0]
    h, big_l = S5_GROUP, S5_CHUNK
    lane = lax.broadcasted_iota(jnp.int32, (h, LANES), 1)
    for g in range(gb):
        rows = slice(g * h, (g + 1) * h)
        lam_re = jnp.minimum(are_ref[g], -S5_EIG_CLIP)
        lam_im = aim_ref[g]
        dt = jnp.exp(ldt_ref[g])
        mag = jnp.exp(lam_re * dt)
        ar = mag * jnp.cos(lam_im * dt)
        ai = mag * jnp.sin(lam_im * dt)
        den = lam_re * lam_re + lam_im * lam_im
        z_re = ((ar - 1.0) * lam_re + ai * lam_im) / den
        z_im = (ai * lam_re - (ar - 1.0) * lam_im) / den
        btr, bti = btr_ref[g], bti_ref[g]
        bbr = z_re * btr - z_im * bti
        bbi = z_re * bti + z_im * btr
        cre, cim = cre_ref[g], cim_ref[g]
        pr = jnp.ones_like(ar)
        pi = jnp.zeros_like(ar)
        cp_rows = []
        for tau in range(big_l + 1):
            cp = jnp.concatenate([cre * pr - cim * pi, -(cre * pi + cim * pr)], axis=-1)
            if tau < big_l:
                cp_rows.append(cp)
                ba = jnp.concatenate([bbr * pr - bbi * pi, bbr * pi + bbi * pr], axis=-1)
                bac_ref[tau, rows, :] = ba.astype(bac_ref.dtype)
            if tau >= 1:
                cpc_ref[tau - 1, rows, :] = cp.astype(cpc_ref.dtype)
            if tau == big_l:
                al_ref[0, g:g + 1, :] = jnp.concatenate([pr, pr], axis=-1)
                al_ref[1, g:g + 1, :] = jnp.concatenate([-pi, pi], axis=-1)
            pr, pi = pr * ar - pi * ai, pr * ai + pi * ar
        cp_all = jnp.concatenate(cp_rows, axis=0)
        bb = jnp.concatenate([bbr, bbi], axis=-1)
        krow = lax.dot_general(bb, cp_all, (((1,), (1,)), ((), ())), precision=HIGHEST,
                               preferred_element_type=F32)
        own = (lane >= g * h) & (lane < (g + 1) * h)
        taus_per_vreg = LANES // h
        for tau in range(big_l):
            src = krow[:, (tau // taus_per_vreg) * LANES:(tau // taus_per_vreg + 1) * LANES]
            shift = ((g - tau % taus_per_vreg) * h) % LANES
            moved = src if shift == 0 else pltpu.roll(src, shift, axis=1)
            dt_ref[tau, rows, :] = jnp.where(own, moved, 0.0).astype(dt_ref.dtype)


def _s5_prep(a_re, a_im, log_dt, b_re, b_im, c_re, c_im):
    g, p = a_re.shape
    h, big_l = S5_GROUP, S5_CHUNK
    gb = LANES // h
    assert p == S5_STATE and 2 * p == LANES and g % gb == 0
    row = lambda z: z.astype(F32).reshape(g, 1, p)
    ldt = jnp.broadcast_to(log_dt.astype(F32)[:, None, None], (g, 1, p))
    bt = lambda z: jnp.swapaxes(z.astype(F32), 1, 2)
    vspec = pl.BlockSpec((gb, 1, p), lambda i: (i, 0, 0))
    mspec = pl.BlockSpec((gb, h, p), lambda i: (i, 0, 0))
    op_spec = pl.BlockSpec((None, big_l, LANES, LANES), lambda i: (i, 0, 0, 0))
    op_shape = jax.ShapeDtypeStruct((g // gb, big_l, LANES, LANES), BF16)
    return pl.pallas_call(
        _s5_prep_kernel, grid=(g // gb,),
        in_specs=[vspec, vspec, vspec, mspec, mspec, mspec, mspec],
        out_specs=[op_spec, op_spec, op_spec, pl.BlockSpec((None, 2, gb, LANES), lambda i: (i, 0, 0, 0))],
        out_shape=[op_shape, op_shape, op_shape, jax.ShapeDtypeStruct((g // gb, 2, gb, LANES), F32)],
        compiler_params=_cparams("parallel"), name="s5_prep",
    )(row(a_re), row(a_im), ldt, bt(b_re), bt(b_im), c_re.astype(F32), c_im.astype(F32))


def _s5_main_kernel(a_ref, dt_ref, bac_ref, cpc_ref, al_ref, y_ref,
                    u2_ref, bt_ref, bdw_ref, bdo_ref, w3_ref, w3s_ref, sp3_ref, *, batch, n_chunks):
    h, big_l = S5_GROUP, S5_CHUNK
    gb = LANES // h
    nrows = batch * n_chunks
    nseq = batch * gb

    @pl.when(pl.program_id(0) == 0)
    def _():
        bt_ref[...] = jnp.zeros_like(bt_ref)
        bdw_ref[...] = jnp.zeros_like(bdw_ref)
        bdo_ref[...] = jnp.zeros_like(bdo_ref)

    for s in range(big_l):
        for t in range(s, big_l):
            bt_ref[s * LANES:(s + 1) * LANES, t * LANES:(t + 1) * LANES] = dt_ref[t - s]
        for g in range(gb):
            rows = slice(s * LANES + g * h, s * LANES + (g + 1) * h)
            cols = slice(g * LANES, (g + 1) * LANES)
            bdw_ref[rows, cols] = bac_ref[big_l - 1 - s, g * h:(g + 1) * h, :]
            bdo_ref[rows, cols] = cpc_ref[s, g * h:(g + 1) * h, :]

    for s in range(big_l):
        u2_ref[:, s * LANES:(s + 1) * LANES] = a_ref[pl.ds(s, nrows, stride=big_l), :].astype(BF16)
    u2 = u2_ref[...]

    win = jnp.dot(u2, bdw_ref[...], preferred_element_type=F32)
    for b in range(batch):
        for g in range(gb):
            w3_ref[pl.ds(b * gb + g, n_chunks, stride=nseq), :] = (
                win[b * n_chunks:(b + 1) * n_chunks, g * LANES:(g + 1) * LANES])
    w3s_ref[...] = pltpu.roll(w3_ref[...], LANES // 2, axis=1)
    a1 = jnp.concatenate([al_ref[0]] * batch, axis=0)
    a2 = jnp.concatenate([al_ref[1]] * batch, axis=0)

    def step(c, carry):
        st, st_sw = carry
        base = pl.multiple_of(c * nseq, nseq)
        sp3_ref[pl.ds(base, nseq), :] = st
        new = a1 * st + a2 * st_sw + w3_ref[pl.ds(base, nseq), :]
        new_sw = a1 * st_sw - a2 * st + w3s_ref[pl.ds(base, nseq), :]
        return new, new_sw

    zero = jnp.zeros((nseq, LANES), F32)
    lax.fori_loop(0, n_chunks, step, (zero, zero), unroll=True)
    sprev =jnp.concatenate(
        [jnp.concatenate([sp3_ref[pl.ds(b * gb + g, n_chunks, stride=nseq), :] for b in range(batch)], axis=0)
         for g in range(gb)], axis=1).astype(BF16)

    nt = (((1,), (1,)), ((), ()))
    pair = 2 * LANES
    for tp in range(big_l * LANES // pair):
        cols = slice(tp * pair, (tp + 1) * pair)
        live = (tp + 1) * pair
        y2 = (jnp.dot(u2_ref[:, 0:live], bt_ref[0:live, cols], preferred_element_type=F32)
              + lax.dot_general(sprev, bdo_ref[cols, :], nt, preferred_element_type=F32))
        y_ref[pl.ds(2 * tp, nrows, stride=big_l), :] = y2[:, 0:LANES]
        y_ref[pl.ds(2 * tp + 1, nrows, stride=big_l), :] = y2[:, LANES:pair]


def _s5_core(a, batch, dt_op, bac, cpc, al):
    m, d = a.shape
    t = m // batch
    big_l = S5_CHUNK
    gb = LANES // S5_GROUP
    n_chunks = t // big_l
    nrows = batch * n_chunks
    width = big_l * LANES
    tok = pl.BlockSpec((m, LANES), lambda i: (0, i))
    op_spec = pl.BlockSpec((None, big_l, LANES, LANES), lambda i: (i, 0, 0, 0))
    return pl.pallas_call(
        functools.partial(_s5_main_kernel, batch=batch, n_chunks=n_chunks),
        grid=(d // LANES,),
        in_specs=[tok, op_spec, op_spec, op_spec, pl.BlockSpec((None, 2, gb, LANES), lambda i: (i, 0, 0, 0))],
        out_specs=tok,
        out_shape=jax.ShapeDtypeStruct((m, d), F32),
        scratch_shapes=[pltpu.VMEM((nrows, width), BF16),
                        pltpu.VMEM((width, width), BF16),
                        pltpu.VMEM((width, gb * LANES), BF16),
                        pltpu.VMEM((width, gb * LANES), BF16),
                        pltpu.VMEM((n_chunks * batch * gb, LANES), F32),
                        pltpu.VMEM((n_chunks * batch * gb, LANES), F32),
                        pltpu.VMEM((n_chunks * batch * gb, LANES), F32)],
        compiler_params=_cparams("arbitrary"), name="s5_main",
    )(a, dt_op, bac, cpc, al)


def _s5_act_kernel(y_ref, a_ref, d_ref, z_ref):
    z_ref[...] = jax.nn.gelu(y_ref[...] + d_ref[...] * a_ref[...]).astype(z_ref.dtype)


def _s5_act(y, a, d_skip, bm=256):
    m, d = y.shape
    bm = _tile(m, bm)
    row = pl.BlockSpec((bm, d), lambda i: (i, 0))
    vec = pl.BlockSpec((1, d), lambda i: (0, 0))
    return pl.pallas_call(
        _s5_act_kernel, grid=(m // bm,), in_specs=[row, row, vec], out_specs=row,
        out_shape=jax.ShapeDtypeStruct((m, d), BF16),
        compiler_params=_cparams("parallel"), name="s5_act",
    )(y, a, d_skip.astype(F32).reshape(1, d))


def _hgrn_stage1(q, k, cum, crow, vb, st, diag_mask, with_diag):
    nt = (((1,), (1,)), ((), ()))
    tn = (((0,), (0,)), ((), ()))
    r15, r31, r47, r63 = crow(15), crow(31), crow(47), crow(63)

    def zrows(n):
        return jnp.zeros((n, HGRN_HEAD), F32)

    q0 = q * jnp.exp(cum)
    q15 = q[16:32] * jnp.exp(cum[16:32] - r15)
    q31 = q[32:64] * jnp.exp(cum[32:64] - r31)
    q47 = q[48:64] * jnp.exp(cum[48:64] - r47)
    k63 = k * jnp.exp(r63 - cum)
    k31 = k[0:32] * jnp.exp(r31 - cum[0:32])
    k15 = k[0:16] * jnp.exp(r15 - cum[0:16])
    k47 = k[32:48] * jnp.exp(r47 - cum[32:48])
    o_state = lax.dot_general(q0.astype(BF16), st.astype(BF16), nt, preferred_element_type=F32)
    st_new = st * jnp.exp(r63) + lax.dot_general(vb, k63.astype(BF16), tn, preferred_element_type=F32)
    q_cat = jnp.concatenate([
        jnp.concatenate([zrows(32), q31], axis=0),
        jnp.concatenate([zrows(16), q15, zrows(32)], axis=0),
        jnp.concatenate([zrows(48), q47], axis=0)], axis=-1).astype(BF16)
    k_cat = jnp.concatenate([
        jnp.concatenate([k31, zrows(32)], axis=0),
        jnp.concatenate([k15, zrows(48)], axis=0),
        jnp.concatenate([zrows(32), k47, zrows(16)], axis=0)], axis=-1).astype(BF16)
    scores = lax.dot_general(q_cat, k_cat, nt, preferred_element_type=F32)
    if with_diag:
        q_d = jnp.concatenate([q0[0:16], q15, q31[0:16], q47], axis=0).astype(BF16)
        k_d = jnp.concatenate([
            k[0:16] * jnp.exp(jnp.minimum(-cum[0:16], HGRN_SAFE_DECAY)),
            k[16:32] * jnp.exp(jnp.minimum(r15 - cum[16:32], HGRN_SAFE_DECAY)),
            k[32:48] * jnp.exp(jnp.minimum(r31 - cum[32:48], HGRN_SAFE_DECAY)),
            k[48:64] * jnp.exp(jnp.minimum(r47 - cum[48:64], HGRN_SAFE_DECAY))], axis=0).astype(BF16)
        scores = scores + jnp.where(diag_mask, lax.dot_general(q_d, k_d, nt, preferred_element_type=F32), 0.0)
    return o_state, st_new, scores


def _hgrn_exact_diag(qtile, krow, crow):
    c_len, sub = HGRN_CHUNK, HGRN_SUB
    row8 = lax.broadcasted_iota(jnp.int32, (SUBLANES, HGRN_HEAD), 0)
    lane8 = lax.broadcasted_iota(jnp.int32, (SUBLANES, HGRN_HEAD), 1)
    pairs = [(t0, list(range((t0 // sub) * sub, t0 + SUBLANES))) for t0 in range(0, c_len, SUBLANES)]
    e_tiles = []
    for t0, s_list in pairs:
        qt = qtile(t0)
        ct = jnp.concatenate([crow(t0 + i) for i in range(SUBLANES)], axis=0)
        for s in s_list:
            e = qt * krow(s) * jnp.exp(ct - crow(s))
            if s >= t0:
                e = jnp.where(row8 >= s - t0, e, 0.0)
            e_tiles.append(e)
    e_all = jnp.concatenate(e_tiles, axis=0).astype(BF16)
    sums = jnp.dot(e_all, jnp.ones((HGRN_HEAD, HGRN_HEAD), BF16), preferred_element_type=F32)
    diag_tiles = []
    idx = 0
    for t0, s_list in pairs:
        acc = jnp.zeros((SUBLANES, HGRN_HEAD), F32)
        for s in s_list:
            acc = jnp.where(lane8 == s, sums[idx * SUBLANES:(idx + 1) * SUBLANES], acc)
            idx += 1
        diag_tiles.append(acc)
    return jnp.concatenate(diag_tiles, axis=0)[:, 0:c_len]


def _hgrn_finish(o, scores, vb, gn, sg):
    o = o + jnp.dot(scores.astype(BF16), vb, preferred_element_type=F32)
    o = o * lax.rsqrt(jnp.mean(o * o, axis=-1, keepdims=True) + RMS_EPS)
    return o * gn * sg.astype(F32)


def _hgrn_core_kernel(q_ref, k_ref, lf_ref, v_ref, sg_ref, gn_ref, o_ref, st_ref, cum_ref, kc_ref, cc_ref,
                      tri_ref, *, heads):
    c_len, sub, hd_dim = HGRN_CHUNK, HGRN_SUB, HGRN_HEAD
    tb = q_ref.shape[0]
    n_chunks = tb // c_len
    assert c_len == 4 * sub

    ri = lax.broadcasted_iota(jnp.int32, (c_len, c_len), 0)
    ci = lax.broadcasted_iota(jnp.int32, (c_len, c_len), 1)
    diag_mask = (ri >= ci) & ((ri // sub) == (ci // sub))
    lanes = [slice(hd * hd_dim, (hd + 1) * hd_dim) for hd in range(heads)]

    @pl.when(pl.program_id(2) == 0)
    def _():
        st_ref[...] = jnp.zeros_like(st_ref)
        rb = lax.broadcasted_iota(jnp.int32, (tb, tb), 0)
        cb = lax.broadcasted_iota(jnp.int32, (tb, tb), 1)
        tri = ((rb >= cb) & ((rb // c_len) == (cb // c_len))).astype(BF16)
        tri_ref[...] = jnp.concatenate([tri, tri, tri], axis=1)

    lf = lf_ref[...]
    lf_hi = lf.astype(BF16)
    rest = lf - lf_hi.astype(F32)
    lf_mid = rest.astype(BF16)
    lf_lo = (rest - lf_mid.astype(F32)).astype(BF16)
    cum_ref[...] = jnp.dot(tri_ref[...], jnp.concatenate([lf_hi, lf_mid, lf_lo], axis=0),
                           preferred_element_type=F32)
    block_decay = -jnp.sum(lf.reshape(tb // sub, sub, lf.shape[1]), axis=1)
    exact_needed = jnp.max(block_decay) > HGRN_SAFE_DECAY

    @pl.when(jnp.logical_not(exact_needed))
    def _():
        for c in range(n_chunks):
            r0 = c * c_len
            rows = slice(r0, r0 + c_len)
            for hd, ls in enumerate(lanes):
                vb = v_ref[rows, ls].astype(BF16)
                o, st_new, scores = _hgrn_stage1(
                    q_ref[rows, ls], k_ref[rows, ls], cum_ref[rows, ls],
                    lambda i, r0=r0, ls=ls: cum_ref[r0 + i:r0 + i + 1, ls],
                    vb, st_ref[hd], diag_mask, True)
                st_ref[hd] = st_new
                o_ref[rows, ls] = _hgrn_finish(o, scores, vb, gn_ref[:, ls], sg_ref[rows, ls]).astype(o_ref.dtype)

    @pl.when(exact_needed)
    def _():
        def chunk_body(c, carry):
            r0 = pl.multiple_of(c * c_len, c_len)
            rows = pl.ds(r0, c_len)
            cc_ref[...] = cum_ref[rows, :]
            kc_ref[...] = k_ref[rows, :]
            for hd, ls in enumerate(lanes):
                vb = v_ref[rows, ls].astype(BF16)
                o, st_new, scores = _hgrn_stage1(
                    q_ref[rows, ls], kc_ref[:, ls], cc_ref[:, ls], lambda i, ls=ls: cc_ref[i:i + 1, ls],
                    vb, st_ref[hd], diag_mask, False)
                st_ref[hd] = st_new
                scores = scores + _hgrn_exact_diag(
                    lambda t0, ls=ls: q_ref[pl.ds(r0 + t0, SUBLANES), ls],
                    lambda s, ls=ls: kc_ref[s:s + 1, ls],
                    lambda s, ls=ls: cc_ref[s:s + 1, ls])
                o_ref[rows, ls] = _hgrn_finish(o, scores, vb, gn_ref[:, ls], sg_ref[rows, ls]).astype(o_ref.dtype)
            return carry

        lax.fori_loop(0, n_chunks, chunk_body, 0)


def _hgrn_core(q, k, log_f, v, sg, g_norm, batch, tb=256, heads=32):
    m, d = q.shape
    t = m // batch
    tb = _tile(t, tb)
    heads = _tile(d // HGRN_HEAD, heads)
    wl = heads * HGRN_HEAD
    nt = t // tb
    blk = pl.BlockSpec((tb, wl), lambda b, j, i: (b * nt + i, j))
    vec = pl.BlockSpec((1, wl), lambda b, j, i: (0, j))
    return pl.pallas_call(
        functools.partial(_hgrn_core_kernel, heads=heads),
        grid=(batch, d // wl, nt),
        in_specs=[blk, blk, blk, blk, blk, vec],
        out_specs=blk,
        out_shape=jax.ShapeDtypeStruct((m, d), BF16),
        scratch_shapes=[pltpu.VMEM((heads, HGRN_HEAD, HGRN_HEAD), F32),
                        pltpu.VMEM((tb, wl), F32),
                        pltpu.VMEM((HGRN_CHUNK, wl), F32),
                        pltpu.VMEM((HGRN_CHUNK, wl), F32),
                        pltpu.VMEM((tb, 3 * tb), BF16)],
        compiler_params=_cparams("parallel", "parallel", "arbitrary"), name="hgrn_core",
    )(q, k, log_f, v, sg, g_norm.astype(F32).reshape(1, d))


def kernel(x, norm_gains, s5_a_re, s5_a_im, s5_log_dt, s5_b_re, s5_b_im, s5_c_re, s5_c_im, s5_d,
           s5_w_glu, hgrn_w_in, hgrn_lb_logits, hgrn_g_norm, hgrn_w_out, ffn_w_gate_up, ffn_w_down):
    batch, seq, d = x.shape
    depth = norm_gains.shape[0]
    d_ff = ffn_w_down.shape[1]
    m = batch * seq
    gains = norm_gains.astype(F32)
    lower_bounds = _lower_bounds(hgrn_lb_logits)

    h = x.reshape(m, d).astype(F32)
    a = _norm(h, gains[0, 0], F32)
    f_out = None
    for layer in range(depth):
        j = layer // 2
        if layer % 2 == 0:
            if f_out is not None:
                h, a = _resid_norm(h, f_out, gains[layer - 1, 3], gains[layer, 0], F32)
            s5_ops = _s5_prep(s5_a_re[j], s5_a_im[j], s5_log_dt[j], s5_b_re[j], s5_b_im[j],
                              s5_c_re[j], s5_c_im[j])
            y = _s5_core(a, batch, *s5_ops)
            z = _s5_act(y, a, s5_d[j])
            (mix,) = _matmul(z, s5_w_glu, j, (0, d), d, _ep_glu, (BF16,), bm=2048, name="s5_glu")
        else:
            h, a, q = _matmul_rows(_rows_resid_norm, (h, f_out), (gains[layer - 1, 3], gains[layer, 0]),
                                   (F32, BF16), hgrn_w_in, j, (0,), d, _ep_silu, (F32,), bn=512, name="hgrn_q")
            log_f, key, val, sg = _matmul(a, hgrn_w_in, j, (d, 2 * d, 3 * d), d, _ep_hgrn_fvg, (F32, F32, F32, BF16),
                                          vecs=(lower_bounds[layer],), m_split=2, name="hgrn_fvg")
            o = _hgrn_core(q, key, log_f, val, sg, hgrn_g_norm[j], batch)
            (mix,) = _matmul(o, hgrn_w_out, j, (0,), d, _ep_identity, (BF16,), bn=512, name="hgrn_out")
        h, a_ffn = _resid_norm(h, mix, gains[layer, 1], gains[layer, 2], BF16)
        (act,) = _matmul(a_ffn, ffn_w_gate_up, layer, (0, d_ff), d_ff, _ep_swiglu, (BF16,), bm=2048, name="ffn_up")
        (f_out,) = _matmul(act, ffn_w_down, layer, (0,), d, _ep_identity, (BF16,), x_buffers=1, name="ffn_down")
    h, _ = _resid_norm(h, f_out, gains[depth - 1, 3], None, None)
    return h.reshape(batch, seq, d).astype(x.dtype)
```
